```python
import jax
import jax.numpy as jnp
from jax import lax
import numpy as np

D_MODEL = 1024
BATCH = 4
SEQ = 8192
DEPTH = 2

GRID_W = 64
CTX_LEN = 256
N_MIXERS = 2
N_ATTN_LAYERS = (DEPTH + N_MIXERS - 1) // N_MIXERS
N_RWKV_LAYERS = DEPTH // N_MIXERS
N_MOD = 6
NORM_EPS = 1e-6

N_HEADS = 16
N_KV_HEADS = 4
HEAD_DIM = D_MODEL // N_HEADS
Q_DIM = N_HEADS * HEAD_DIM
KV_DIM = N_KV_HEADS * HEAD_DIM
ROPE_THETA = 10000.0
Q_BLOCK = 128

RWKV_HEAD = 64
RWKV_HEADS = D_MODEL // RWKV_HEAD
DECAY_LORA = 64
AAA_LORA = 64
GATE_LORA = 160
GN_EPS = 64e-5
N_DIRS = 2

D_FF = 2816
CONV_W = 3

kernel_name = 'hybrid_attn_rwkv7_convffn_dit'


def rms_norm(x):
    xf = x.astype(jnp.float32)
    return (xf * lax.rsqrt(jnp.mean(xf * xf, axis=-1, keepdims=True) + NORM_EPS)).astype(x.dtype)


def modulate(x, shift, scale):
    return rms_norm(x) * (1.0 + scale) + shift


def centred_shift(x):
    xp = jnp.pad(x, ((0, 0), (1, 1), (0, 0)))
    return 0.5 * (xp[:, :-2] + xp[:, 2:])


def centred_dwconv(x, w, b):
    t = x.shape[1]
    p = CONV_W // 2
    xp = jnp.pad(x, ((0, 0), (p, p), (0, 0)))
    return sum(w[j] * xp[:, j:j + t] for j in range(CONV_W)) + b


def axial_rope_tables(n_tokens):
    rows = n_tokens // GRID_W
    row = jnp.repeat(jnp.arange(rows, dtype=jnp.float32), GRID_W)
    col = jnp.tile(jnp.arange(GRID_W, dtype=jnp.float32), rows)
    n_freq = HEAD_DIM // 4
    inv_freq = ROPE_THETA ** (-jnp.arange(n_freq, dtype=jnp.float32) / n_freq)
    ang = jnp.concatenate([row[:, None] * inv_freq, col[:, None] * inv_freq], axis=-1)
    return jnp.cos(ang), jnp.sin(ang)


def apply_rope(x, cos, sin):
    xp = x.reshape(*x.shape[:-1], HEAD_DIM // 2, 2)
    x0, x1 = xp[..., 0], xp[..., 1]
    c = cos[None, :, None, :].astype(x.dtype)
    s = sin[None, :, None, :].astype(x.dtype)
    return jnp.stack([x0 * c - x1 * s, x0 * s + x1 * c], axis=-1).reshape(x.shape)


def gqa_attend(q, k, v):
    s = jnp.einsum('bqhgd,bkhd->bhgqk', q, k).astype(jnp.float32) * (HEAD_DIM ** -0.5)
    p = jax.nn.softmax(s, axis=-1).astype(v.dtype)
    return jnp.einsum('bhgqk,bkhd->bqhgd', p, v)


def attention_mixer(h_lat, h_ctx, w_qkv, q_gain, k_gain, w_o, cos, sin, need_ctx_out):
    n_group = N_HEADS // N_KV_HEADS

    def project(h):
        b, t, _ = h.shape
        q, k, v = jnp.split(h @ w_qkv, [Q_DIM, Q_DIM + KV_DIM], axis=-1)
        q = rms_norm(q.reshape(b, t, N_HEADS, HEAD_DIM)) * q_gain
        k = rms_norm(k.reshape(b, t, N_KV_HEADS, HEAD_DIM)) * k_gain
        return q, k, v.reshape(b, t, N_KV_HEADS, HEAD_DIM)

    q_l, k_l, v_l = project(h_lat)
    q_c, k_c, v_c = project(h_ctx)
    q_l = apply_rope(q_l, cos, sin)
    k_l = apply_rope(k_l, cos, sin)
    k_all = jnp.concatenate([k_c, k_l], axis=1)
    v_all = jnp.concatenate([v_c, v_l], axis=1)

    b, s, _ = h_lat.shape
    n_blk = s // Q_BLOCK
    q_blk = jnp.moveaxis(q_l.reshape(b, n_blk, Q_BLOCK, N_KV_HEADS, n_group, HEAD_DIM), 1, 0)
    o_l = lax.map(lambda qb: gqa_attend(qb, k_all, v_all), q_blk)
    o_l = jnp.moveaxis(o_l, 0, 1).reshape(b, s, Q_DIM) @ w_o

    o_c = None
    if need_ctx_out:
        c_len = h_ctx.shape[1]
        o_c = gqa_attend(q_c.reshape(b, c_len, N_KV_HEADS, n_group, HEAD_DIM), k_c, v_c)
        o_c = o_c.reshape(b, c_len, Q_DIM) @ w_o
    return o_l, o_c


def rwkv_heads(t):
    return t.reshape(t.shape[0], t.shape[1], RWKV_HEADS, RWKV_HEAD)


def l2_normalize(x):
    xf = x.astype(jnp.float32)
    return (xf / jnp.maximum(jnp.sqrt(jnp.sum(xf * xf, axis=-1, keepdims=True)), 1e-12)).astype(x.dtype)


def wkv7_scan(state0, decay, k, v, a, b, r, reverse):
    emit = r is not None
    seq = (decay, k, v, a, b) + ((r,) if emit else ())
    xs = tuple(jnp.moveaxis(t, 1, 0) for t in seq)

    def step(S, inp):
        d_t, k_t, v_t, a_t, b_t = inp[:5]
        sa = jnp.einsum('bhvk,bhk->bhv', S, a_t)
        S = S * d_t[:, :, None, :] + sa[..., None] * b_t[:, :, None, :] + v_t[..., None] * k_t[:, :, None, :]
        y = jnp.einsum('bhvk,bhk->bhv', S, inp[5]) if emit else None
        return S, y

    S, ys = lax.scan(step, state0, xs, reverse=reverse)
    return S, (jnp.moveaxis(ys, 0, 1) if emit else None)


def rwkv7_mixer(h_lat, h_ctx, mu, w_r, w_k, w_v, w_o, decay_w0, decay_w1, decay_w2,
                aaa_a0, aaa_a1, aaa_a2, gate_g1, gate_g2, k_k, k_a, r_k, gn_w, gn_b, need_ctx_out):
    def scan_inputs(h):
        xx = centred_shift(h) - h
        xw, xk, xv, xa = (h + xx * mu[i] for i in (1, 2, 3, 4))
        k = xk @ w_k
        v = rwkv_heads(xv @ w_v)
        kk = l2_normalize(rwkv_heads(k * k_k))
        dirs = []
        for d in range(N_DIRS):
            w_log = -jax.nn.softplus(-(decay_w0[d] + jnp.tanh(xw @ decay_w1[d]) @ decay_w2[d])) - 0.5
            a = jax.nn.sigmoid(aaa_a0[d] + (xa @ aaa_a1[d]) @ aaa_a2[d])
            decay = rwkv_heads(jnp.exp(-jnp.exp(w_log)))
            k_d = rwkv_heads(k * (1.0 + (a - 1.0) * k_a))
            dirs.append((decay, k_d, -kk, kk * rwkv_heads(a)))
        return xx, v, dirs

    def receptance(h, xx):
        return rwkv_heads((h + xx * mu[0]) @ w_r)

    def readout(h, xx, r, v, dirs, y):
        b, t, _ = h.shape
        mean = jnp.mean(y, axis=-1, keepdims=True)
        var = jnp.mean(jnp.square(y - mean), axis=-1, keepdims=True)
        y_n = ((y - mean) * lax.rsqrt(var + GN_EPS)).reshape(b, t, D_MODEL).astype(h.dtype) * gn_w + gn_b
        bonus = sum(jnp.sum(r * k_d * r_k, axis=-1, keepdims=True) for (_, k_d, _, _) in dirs) * v
        g = jax.nn.sigmoid((h + xx * mu[5]) @ gate_g1) @ gate_g2
        return ((y_n + bonus.reshape(b, t, D_MODEL)) * g) @ w_o

    b = h_lat.shape[0]
    xx_c, v_c, dirs_c = scan_inputs(h_ctx)
    xx_l, v_l, dirs_l = scan_inputs(h_lat)
    r_c = receptance(h_ctx, xx_c) if need_ctx_out else None
    r_l = receptance(h_lat, xx_l)
    state0 = jnp.zeros((b, RWKV_HEADS, RWKV_HEAD, RWKV_HEAD), jnp.float32)
    ys_l, ys_c = [], []
    for d in range(N_DIRS):
        reverse = d == 1
        dec_c, kd_c, a_c, b_c = dirs_c[d]
        s_ctx, y_c = wkv7_scan(state0, dec_c, kd_c, v_c, a_c, b_c, r_c, reverse)
        dec_l, kd_l, a_l, b_l = dirs_l[d]
        _, y_l = wkv7_scan(s_ctx, dec_l, kd_l, v_l, a_l, b_l, r_l, reverse)
        ys_l.append(y_l)
        ys_c.append(y_c)
    o_l = readout(h_lat, xx_l, r_l, v_l, dirs_l, sum(ys_l))
    o_c = readout(h_ctx, xx_c, r_c, v_c, dirs_c, sum(ys_c)) if need_ctx_out else None
    return o_l, o_c


def conv_ffn(h, w_up, conv_w, conv_b, w_down):
    u = centred_dwconv(h @ w_up, conv_w, conv_b)
    gate, val = jnp.split(u, 2, axis=-1)
    return (jax.nn.silu(gate) * val) @ w_down


def setup_inputs(seed: int = 0) -> dict:
    key = jax.random.key(seed)
    ks = iter(jax.random.split(key, 40))
    D = D_MODEL

    def nrm(shape, scale):
        return jax.random.normal(next(ks), shape, jnp.float32) * scale

    def gain(shape):
        return 1.0 + nrm(shape, 0.05)

    return {
        'x': nrm((BATCH, SEQ, D), 1.0),
        'c': nrm((BATCH, D), 1.0),
        'ctx': nrm((BATCH, CTX_LEN, D), 1.0),
        'c_ctx': nrm((D,), 1.0),
        'ada_w': nrm((DEPTH, D, N_MOD * D), 0.5 * D ** -0.5),
        'ada_b': nrm((DEPTH, N_MOD * D), 0.02),
        'attn_w_qkv': nrm((N_ATTN_LAYERS, D, Q_DIM + 2 * KV_DIM), D ** -0.5),
        'attn_q_gain': gain((N_ATTN_LAYERS, HEAD_DIM)),
        'attn_k_gain': gain((N_ATTN_LAYERS, HEAD_DIM)),
        'attn_w_o': nrm((N_ATTN_LAYERS, Q_DIM, D), Q_DIM ** -0.5),
        'rwkv_mu': jax.random.uniform(next(ks), (N_RWKV_LAYERS, 6, D), jnp.float32),
        'rwkv_w_r': nrm((N_RWKV_LAYERS, D, D), D ** -0.5),
        'rwkv_w_k': nrm((N_RWKV_LAYERS, D, D), D ** -0.5),
        'rwkv_w_v': nrm((N_RWKV_LAYERS, D, D), D ** -0.5),
        'rwkv_w_o': nrm((N_RWKV_LAYERS, D, D), D ** -0.5),
        'rwkv_decay_w0': jax.random.uniform(next(ks), (N_RWKV_LAYERS, N_DIRS, D), jnp.float32, minval=-7.0, maxval=-2.0),
        'rwkv_decay_w1': nrm((N_RWKV_LAYERS, N_DIRS, D, DECAY_LORA), D ** -0.5),
        'rwkv_decay_w2': nrm((N_RWKV_LAYERS, N_DIRS, DECAY_LORA, D), 0.1 * DECAY_LORA ** -0.5),
        'rwkv_aaa_a0': nrm((N_RWKV_LAYERS, N_DIRS, D), 0.1),
        'rwkv_aaa_a1': nrm((N_RWKV_LAYERS, N_DIRS, D, AAA_LORA), D ** -0.5),
        'rwkv_aaa_a2': nrm((N_RWKV_LAYERS, N_DIRS, AAA_LORA, D), 0.1 * AAA_LORA ** -0.5),
        'rwkv_gate_g1': nrm((N_RWKV_LAYERS, D, GATE_LORA), D ** -0.5),
        'rwkv_gate_g2': nrm((N_RWKV_LAYERS, GATE_LORA, D), GATE_LORA ** -0.5),
        'rwkv_k_k': 0.85 + nrm((N_RWKV_LAYERS, D), 0.05),
        'rwkv_k_a': gain((N_RWKV_LAYERS, D)),
        'rwkv_r_k': nrm((N_RWKV_LAYERS, RWKV_HEADS, RWKV_HEAD), 0.1),
        'rwkv_gn_w': gain((N_RWKV_LAYERS, D)),
        'rwkv_gn_b': nrm((N_RWKV_LAYERS, D), 0.02),
        'ffn_w_up': nrm((DEPTH, D, 2 * D_FF), D ** -0.5),
        'ffn_conv_w': nrm((DEPTH, CONV_W, 2 * D_FF), CONV_W ** -0.5),
        'ffn_conv_b': nrm((DEPTH, 2 * D_FF), 0.02),
        'ffn_w_down': nrm((DEPTH, D_FF, D), D_FF ** -0.5),
        'final_gain': gain((D,)),
    }


def reference(x, c, ctx, c_ctx, ada_w, ada_b, attn_w_qkv, attn_q_gain, attn_k_gain, attn_w_o,
              rwkv_mu, rwkv_w_r, rwkv_w_k, rwkv_w_v, rwkv_w_o, rwkv_decay_w0, rwkv_decay_w1, rwkv_decay_w2,
              rwkv_aaa_a0, rwkv_aaa_a1, rwkv_aaa_a2, rwkv_gate_g1, rwkv_gate_g2, rwkv_k_k, rwkv_k_a, rwkv_r_k,
              rwkv_gn_w, rwkv_gn_b, ffn_w_up, ffn_conv_w, ffn_conv_b, ffn_w_down, final_gain):
    cos, sin = axial_rope_tables(x.shape[1])
    x_lat, x_ctx = x, ctx
    for i in range(DEPTH):
        ctx_out = i < DEPTH - 1
        j = i // N_MIXERS
        mod_l = (jax.nn.silu(c) @ ada_w[i] + ada_b[i])[:, None, :]
        mod_c = (jax.nn.silu(c_ctx) @ ada_w[i] + ada_b[i])[None, None, :]
        sh1_l, sc1_l, g1_l, sh2_l, sc2_l, g2_l = jnp.split(mod_l, N_MOD, axis=-1)
        sh1_c, sc1_c, g1_c, sh2_c, sc2_c, g2_c = jnp.split(mod_c, N_MOD, axis=-1)
        h_l = modulate(x_lat, sh1_l, sc1_l)
        h_c = modulate(x_ctx, sh1_c, sc1_c)
        if i % N_MIXERS == 0:
            y_l, y_c = attention_mixer(h_l, h_c, attn_w_qkv[j], attn_q_gain[j], attn_k_gain[j], attn_w_o[j],
                                       cos, sin, ctx_out)
        else:
            y_l, y_c = rwkv7_mixer(h_l, h_c, rwkv_mu[j], rwkv_w_r[j], rwkv_w_k[j], rwkv_w_v[j], rwkv_w_o[j],
                                   rwkv_decay_w0[j], rwkv_decay_w1[j], rwkv_decay_w2[j],
                                   rwkv_aaa_a0[j], rwkv_aaa_a1[j], rwkv_aaa_a2[j],
                                   rwkv_gate_g1[j], rwkv_gate_g2[j], rwkv_k_k[j], rwkv_k_a[j], rwkv_r_k[j],
                                   rwkv_gn_w[j], rwkv_gn_b[j], ctx_out)
        x_lat = x_lat + g1_l * y_l
        x_lat = x_lat + g2_l * conv_ffn(modulate(x_lat, sh2_l, sc2_l),
                                        ffn_w_up[i], ffn_conv_w[i], ffn_conv_b[i], ffn_w_down[i])
        if ctx_out:
            x_ctx = x_ctx + g1_c * y_c
            x_ctx = x_ctx + g2_c * conv_ffn(modulate(x_ctx, sh2_c, sc2_c),
                                            ffn_w_up[i], ffn_conv_w[i], ffn_conv_b[i], ffn_w_down[i])
    return rms_norm(x_lat) * final_gain
```

```python
import functools
import math

import jax
import jax.numpy as jnp
from jax import lax
from jax.experimental import pallas as pl
from jax.experimental.pallas import tpu as pltpu

F32 = jnp.float32
BF16 = jnp.bfloat16

LANES = 128
SUBLANES = 8
VMEM_LIMIT_BYTES = 56 * 1024 * 1024

N_HEADS = 16
N_KV_HEADS = 4
HEAD_DIM = 64
GROUP = N_HEADS // N_KV_HEADS
RWKV_HEAD = 64
GRID_W = 64
ROPE_THETA = 10000.0
NORM_EPS = 1e-6
GN_EPS = 64e-5
N_MOD = 6
MOD_ROWS = 8

TOKEN_TILE = 256
RWKV_TILE = 128
Q_TILE = 256
KV_TILE = 512
FF_CHUNK = 256
CHUNK = 64
HEADS_PER_GROUP = 4
HALO = SUBLANES


def _cparams(sem):
    return pltpu.CompilerParams(dimension_semantics=sem, vmem_limit_bytes=VMEM_LIMIT_BYTES)


def _dot(a, b):
    return jnp.dot(a.astype(BF16), b.astype(BF16), preferred_element_type=F32)


def _dot_nt(a, b):
    return lax.dot_general(a.astype(BF16), b.astype(BF16), (((1,), (1,)), ((), ())),
                           preferred_element_type=F32)


def _dot_tn(a, b):
    return lax.dot_general(a.astype(BF16), b.astype(BF16), (((0,), (0,)), ((), ())),
                           preferred_element_type=F32)


def _split(a):
    hi = a.astype(BF16)
    lo = (a - hi.astype(F32)).astype(BF16)
    return hi, lo


def _dot_split_lhs(a, b_bf16):
    hi, lo = _split(a)
    return (jnp.dot(hi, b_bf16, preferred_element_type=F32)
            + jnp.dot(lo, b_bf16, preferred_element_type=F32))


def _dot_split_both(a, b):
    ah, al = _split(a)
    bh, bl = _split(b)
    d = functools.partial(jnp.dot, preferred_element_type=F32)
    return d(ah, bh) + (d(ah, bl) + d(al, bh))


def _sigmoid(x):
    return 1.0 / (1.0 + jnp.exp(-x))


def _rms_norm(x):
    return x * lax.rsqrt(jnp.mean(x * x, axis=-1, keepdims=True) + NORM_EPS)


def _modulate(x, shift, scale):
    return _rms_norm(x) * (1.0 + scale) + shift


def _const_spec(shape):
    nd = len(shape)
    return pl.BlockSpec(shape, lambda *_: (0,) * nd)


def _ada_kernel(c_ref, w_ref, b_ref, o_ref):
    cc = c_ref[...]
    o_ref[0, 0] = _dot_split_both(cc * _sigmoid(cc), w_ref[0]) + b_ref[0, 0]


def _ada_mods(cc, ada_w, ada_b):
    depth, d, _ = ada_w.shape
    out = pl.pallas_call(
        _ada_kernel,
        grid=(depth, N_MOD),
        in_specs=[
            pl.BlockSpec((MOD_ROWS, d), lambda i, n: (0, 0)),
            pl.BlockSpec((1, d, d), lambda i, n: (i, 0, n)),
            pl.BlockSpec((1, 1, 1, d), lambda i, n: (i, n, 0, 0)),
        ],
        out_specs=pl.BlockSpec((1, 1, MOD_ROWS, d), lambda i, n: (i, n, 0, 0)),
        out_shape=jax.ShapeDtypeStruct((depth, N_MOD, MOD_ROWS, d), F32),
        compiler_params=_cparams(("arbitrary", "arbitrary")),
    )(cc, ada_w, ada_b.reshape(depth, N_MOD, 1, d))
    return out.transpose(0, 2, 1, 3)


def _rope(x, cos, sin):
    width = x.shape[1]
    reps = width // LANES
    cc = jnp.concatenate([cos] * reps, axis=1)
    ss = jnp.concatenate([sin] * reps, axis=1)
    lane = lax.broadcasted_iota(jnp.int32, x.shape, 1)
    partner = jnp.where((lane & 1) == 0, pltpu.roll(x, width - 1, 1), pltpu.roll(x, 1, 1))
    return x * cc + partner * ss


def _attn_pre_kernel(x_ref, mod_ref, w_ref, qg_ref, kg_ref, gq_ref, gk_ref, cos_ref, sin_ref,
                     q_ref, kt_ref, v_ref, *, q_dim, kv_dim):
    mod = mod_ref[0]
    h = _modulate(x_ref[0], mod[0:1], mod[1:2])
    qkv = _dot(h, w_ref[...])
    q = qkv[:, :q_dim]
    k = qkv[:, q_dim:q_dim + kv_dim]
    v = qkv[:, q_dim + kv_dim:]
    q = q * lax.rsqrt(_dot_split_lhs(q * q, gq_ref[...]) + NORM_EPS) * qg_ref[...]
    k = k * lax.rsqrt(_dot_split_lhs(k * k, gk_ref[...]) + NORM_EPS) * kg_ref[...]
    cos = cos_ref[...]
    sin = sin_ref[...]
    q = _rope(q, cos, sin) * (HEAD_DIM ** -0.5)
    k = _rope(k, cos, sin)
    q_ref[0] = q.astype(BF16)
    kt_ref[0] = k.T.astype(BF16)
    rows = v.shape[0]
    one_col = (lax.broadcasted_iota(jnp.int32, (rows, HEAD_DIM), 1) == 0).astype(F32)
    for g in range(N_KV_HEADS):
        v_ref[0, g] = jnp.concatenate([v[:, g * HEAD_DIM:(g + 1) * HEAD_DIM], one_col], axis=1).astype(BF16)


def _attn_pre(xa, mods, w_qkv, q_gain, k_gain, cos_t, sin_t, n_ctx_tiles):
    b, t, d = xa.shape
    q_dim = N_HEADS * HEAD_DIM
    kv_dim = N_KV_HEADS * HEAD_DIM
    tm = TOKEN_TILE
    nb = b
    gq = jnp.kron(jnp.eye(N_HEADS, dtype=F32), jnp.full((HEAD_DIM, HEAD_DIM), 1.0 / HEAD_DIM, F32)).astype(BF16)
    gk = jnp.kron(jnp.eye(N_KV_HEADS, dtype=F32), jnp.full((HEAD_DIM, HEAD_DIM), 1.0 / HEAD_DIM, F32)).astype(BF16)
    mod_map = lambda bi, ti: (jnp.where(ti < n_ctx_tiles, nb, bi), 0, 0)
    return pl.pallas_call(
        functools.partial(_attn_pre_kernel, q_dim=q_dim, kv_dim=kv_dim),
        grid=(b, t // tm),
        in_specs=[
            pl.BlockSpec((1, tm, d), lambda bi, ti: (bi, ti, 0)),
            pl.BlockSpec((1, N_MOD, d), mod_map),
            _const_spec((d, q_dim + 2 * kv_dim)),
            _const_spec((1, q_dim)),
            _const_spec((1, kv_dim)),
            _const_spec((q_dim, q_dim)),
            _const_spec((kv_dim, kv_dim)),
            pl.BlockSpec((tm, LANES), lambda bi, ti: (ti, 0)),
            pl.BlockSpec((tm, LANES), lambda bi, ti: (ti, 0)),
        ],
        out_specs=[
            pl.BlockSpec((1, tm, q_dim), lambda bi, ti: (bi, ti, 0)),
            pl.BlockSpec((1, kv_dim, tm), lambda bi, ti: (bi, 0, ti)),
            pl.BlockSpec((1, N_KV_HEADS, tm, LANES), lambda bi, ti: (bi, 0, ti, 0)),
        ],
        out_shape=[
            jax.ShapeDtypeStruct((b, t, q_dim), BF16),
            jax.ShapeDtypeStruct((b, kv_dim, t), BF16),
            jax.ShapeDtypeStruct((b, N_KV_HEADS, t, LANES), BF16),
        ],
        compiler_params=_cparams(("parallel", "parallel")),
    )(xa, mods, w_qkv.astype(BF16), jnp.tile(q_gain, N_HEADS)[None], jnp.tile(k_gain, N_KV_HEADS)[None],
      gq, gk, cos_t, sin_t)


def _flash_kernel(q_ref, kt_ref, v_ref, o_ref, m_ref, acc_ref, *, n_ctx_q, ctx_len, n_lat_kv):
    qi = pl.program_id(2)
    tq = q_ref.shape[1]
    q = q_ref[0]
    q4 = jnp.concatenate([q[:, h * HEAD_DIM:(h + 1) * HEAD_DIM] for h in range(GROUP)], axis=0)
    m_ref[...] = jnp.full(m_ref.shape, -1e30, F32)
    acc_ref[...] = jnp.zeros(acc_ref.shape, F32)

    def visit(start, size):
        kt = kt_ref[0, :, pl.ds(start, size)]
        s = jnp.dot(q4, kt, preferred_element_type=F32)
        m_prev = m_ref[...]
        m_new = jnp.maximum(m_prev, jnp.max(s, axis=1, keepdims=True))
        p = jnp.exp(s - m_new)
        pv = jnp.dot(p.astype(BF16), v_ref[0, 0, pl.ds(start, size), :], preferred_element_type=F32)
        acc_ref[...] = jnp.exp(m_prev - m_new) * acc_ref[...] + pv
        m_ref[...] = m_new

    visit(0, ctx_len)

    @pl.when(qi >= n_ctx_q)
    def _():
        def body(j, carry):
            visit(pl.multiple_of(ctx_len + j * KV_TILE, LANES), KV_TILE)
            return carry
        lax.fori_loop(0, n_lat_kv, body, 0)

    acc = acc_ref[...]
    o = acc[:, :HEAD_DIM] / acc[:, HEAD_DIM:HEAD_DIM + 1]
    o_ref[0] = jnp.concatenate([o[h * tq:(h + 1) * tq] for h in range(GROUP)], axis=1).astype(BF16)


def _flash(q, kt, v, ctx_len):
    b, t, q_dim = q.shape
    tq = Q_TILE
    gw = GROUP * HEAD_DIM
    kern = functools.partial(_flash_kernel, n_ctx_q=ctx_len // tq, ctx_len=ctx_len,
                             n_lat_kv=(t - ctx_len) // KV_TILE)
    return pl.pallas_call(
        kern,
        grid=(b, N_KV_HEADS, t // tq),
        in_specs=[
            pl.BlockSpec((1, tq, gw), lambda bi, g, qi: (bi, qi, g)),
            pl.BlockSpec((1, HEAD_DIM, t), lambda bi, g, qi: (bi, g, 0)),
            pl.BlockSpec((1, 1, t, LANES), lambda bi, g, qi: (bi, g, 0, 0)),
        ],
        out_specs=pl.BlockSpec((1, tq, gw), lambda bi, g, qi: (bi, qi, g)),
        out_shape=jax.ShapeDtypeStruct((b, t, q_dim), BF16),
        scratch_shapes=[pltpu.VMEM((GROUP * tq, 1), F32), pltpu.VMEM((GROUP * tq, LANES), F32)],
        compiler_params=_cparams(("parallel", "parallel", "arbitrary")),
    )(q, kt, v)


def _attn_out_kernel(x_ref, o_ref, mod_ref, w_ref, out_ref):
    y = jnp.dot(o_ref[0], w_ref[...], preferred_element_type=F32)
    out_ref[0] = x_ref[0] + mod_ref[0][2:3] * y


def _attn_out(xa, o, mods, w_o, n_ctx_tiles):
    b, t, d = xa.shape
    tm = TOKEN_TILE
    nb = b
    mod_map = lambda bi, ti: (jnp.where(ti < n_ctx_tiles, nb, bi), 0, 0)
    return pl.pallas_call(
        _attn_out_kernel,
        grid=(b, t // tm),
        in_specs=[
            pl.BlockSpec((1, tm, d), lambda bi, ti: (bi, ti, 0)),
            pl.BlockSpec((1, tm, o.shape[2]), lambda bi, ti: (bi, ti, 0)),
            pl.BlockSpec((1, N_MOD, d), mod_map),
            _const_spec(w_o.shape),
        ],
        out_specs=pl.BlockSpec((1, tm, d), lambda bi, ti: (bi, ti, 0)),
        out_shape=jax.ShapeDtypeStruct((b, t, d), F32),
        compiler_params=_cparams(("parallel", "parallel")),
    )(xa, o, mods, w_o.astype(BF16))


def _halo_specs(tm, d, n_row_blocks):
    per = tm // HALO
    return [
        pl.BlockSpec((1, tm, d), lambda bi, ti: (bi, ti, 0)),
        pl.BlockSpec((1, HALO, d), lambda bi, ti: (bi, jnp.maximum(ti * per - 1, 0), 0)),
        pl.BlockSpec((1, HALO, d), lambda bi, ti: (bi, jnp.minimum((ti + 1) * per, n_row_blocks - 1), 0)),
    ]


def _edge_keep(ti, n_ctx_tiles, n_tiles):
    first = (ti == 0) | (ti == n_ctx_tiles)
    last = (ti == n_tiles - 1) | (ti == n_ctx_tiles - 1)
    return jnp.where(first, 0.0, 1.0).astype(F32), jnp.where(last, 0.0, 1.0).astype(F32)


def _ffn_kernel(*refs, n_ctx_tiles, n_tiles, n_fc, final):
    if final:
        xm_ref, xp_ref, xn_ref, mod_ref, wup_ref, cw_ref, cb_ref, wdn_ref, fg_ref, out_ref, acc_ref = refs
    else:
        xm_ref, xp_ref, xn_ref, mod_ref, wup_ref, cw_ref, cb_ref, wdn_ref, out_ref, acc_ref = refs
    tm = xm_ref.shape[1]
    ext = tm + 2 * HALO
    mod = mod_ref[0]
    shift, scale, gate = mod[3:4], mod[4:5], mod[5:6]
    keep_prev, keep_next = _edge_keep(pl.program_id(1), n_ctx_tiles, n_tiles)
    xm = xm_ref[0]
    h = jnp.concatenate([_modulate(xp_ref[0], shift, scale) * keep_prev,
                         _modulate(xm, shift, scale),
                         _modulate(xn_ref[0], shift, scale) * keep_next], axis=0).astype(BF16)
    acc_ref[...] = jnp.zeros(acc_ref.shape, F32)

    def conv(u, cw, cb):
        y = cw[0:1] * pltpu.roll(u, 1, 0) + cw[1:2] * u + cw[2:3] * pltpu.roll(u, ext - 1, 0) + cb
        return y[HALO:HALO + tm]

    def body(j, carry):
        ug = jnp.dot(h, wup_ref[j], preferred_element_type=F32)
        uv = jnp.dot(h, wup_ref[n_fc + j], preferred_element_type=F32)
        cg = conv(ug, cw_ref[j], cb_ref[j])
        cv = conv(uv, cw_ref[n_fc + j], cb_ref[n_fc + j])
        act = cg * _sigmoid(cg) * cv
        acc_ref[...] += jnp.dot(act.astype(BF16), wdn_ref[j], preferred_element_type=F32)
        return carry

    lax.fori_loop(0, n_fc, body, 0)
    out = xm + gate * acc_ref[...]
    if final:
        out = _rms_norm(out) * fg_ref[...]
    out_ref[0] = out


def _conv_ffn(xa, mods, w_up, conv_w, conv_b, w_down, n_ctx_tiles, tm, final_gain=None):
    b, t, d = xa.shape
    d_ff = w_down.shape[0]
    n_fc = d_ff // FF_CHUNK
    n_tiles = t // tm
    nb = b
    final = final_gain is not None
    wup = w_up.astype(BF16).reshape(d, 2 * n_fc, FF_CHUNK).transpose(1, 0, 2)
    cw = conv_w.reshape(conv_w.shape[0], 2 * n_fc, FF_CHUNK).transpose(1, 0, 2)
    cb = conv_b.reshape(2 * n_fc, 1, FF_CHUNK)
    wdn = w_down.astype(BF16).reshape(n_fc, FF_CHUNK, d)
    mod_map = lambda bi, ti: (jnp.where(ti < n_ctx_tiles, nb, bi), 0, 0)
    in_specs = _halo_specs(tm, d, t // HALO) + [
        pl.BlockSpec((1, N_MOD, d), mod_map),
        _const_spec(wup.shape), _const_spec(cw.shape), _const_spec(cb.shape), _const_spec(wdn.shape),
    ]
    args = [xa, xa, xa, mods, wup, cw, cb, wdn]
    if final:
        in_specs.append(_const_spec((1, d)))
        args.append(final_gain[None])
    return pl.pallas_call(
        functools.partial(_ffn_kernel, n_ctx_tiles=n_ctx_tiles, n_tiles=n_tiles, n_fc=n_fc, final=final),
        grid=(b, n_tiles),
        in_specs=in_specs,
        out_specs=pl.BlockSpec((1, tm, d), lambda bi, ti: (bi, ti, 0)),
        out_shape=jax.ShapeDtypeStruct((b, t, d), F32),
        scratch_shapes=[pltpu.VMEM((tm, d), F32)],
        compiler_params=_cparams(("parallel", "parallel")),
    )(*args)


def _rwkv_pre_kernel(xm_ref, xp_ref, xn_ref, mod_ref, mu_ref, wr_ref, wk_ref, wv_ref,
                     dw1_ref, dw2_ref, w0_ref, a1_ref, a2_ref, a0_ref, g1_ref, g2_ref,
                     kk_ref, ka_ref, rk_ref, gsum_ref,
                     r_out, v_out, k_out, kk_out, a_out, ld_out, g_out, bv_out,
                     *, n_ctx_tiles, n_tiles):
    tm = xm_ref.shape[1]
    d = xm_ref.shape[2]
    ext = tm + 2 * HALO
    mod = mod_ref[0]
    shift, scale = mod[0:1], mod[1:2]
    keep_prev, keep_next = _edge_keep(pl.program_id(1), n_ctx_tiles, n_tiles)
    h = _modulate(xm_ref[0], shift, scale)
    he = jnp.concatenate([_modulate(xp_ref[0], shift, scale) * keep_prev, h,
                          _modulate(xn_ref[0], shift, scale) * keep_next], axis=0)
    xx = 0.5 * (pltpu.roll(he, 1, 0) + pltpu.roll(he, ext - 1, 0))[HALO:HALO + tm] - h
    mu = mu_ref[...]
    mix = lambda i: h + xx * mu[i:i + 1]
    r = _dot(mix(0), wr_ref[...])
    k = _dot(mix(2), wk_ref[...])
    v = _dot(mix(3), wv_ref[...])
    z = w0_ref[...] + _dot(jnp.tanh(_dot(mix(1), dw1_ref[...])), dw2_ref[...])
    log_decay = -math.exp(-0.5) * _sigmoid(z)
    a = _sigmoid(a0_ref[...] + _dot(_dot(mix(4), a1_ref[...]), a2_ref[...]))
    g = _dot(_sigmoid(_dot(mix(5), g1_ref[...])), g2_ref[...])
    gsum = gsum_ref[...]
    kx = k * kk_ref[...]
    kk = kx / jnp.maximum(jnp.sqrt(_dot_split_lhs(kx * kx, gsum)), 1e-12)
    ka = ka_ref[...]
    rk = r * rk_ref[...]
    bonus = rk * (k * (1.0 + (a[:, :d] - 1.0) * ka)) + rk * (k * (1.0 + (a[:, d:] - 1.0) * ka))
    r_out[0] = r
    v_out[0] = v
    k_out[0] = k
    kk_out[0] = kk
    a_out[0, 0] = a[:, :d]
    a_out[1, 0] = a[:, d:]
    ld_out[0, 0] = log_decay[:, :d]
    ld_out[1, 0] = log_decay[:, d:]
    g_out[0] = g
    bv_out[0] = _dot_split_lhs(bonus, gsum) * v


def _block_diag2(m0, m1):
    z = jnp.zeros_like(m0)
    return jnp.concatenate([jnp.concatenate([m0, z], axis=1), jnp.concatenate([z, m1], axis=1)], axis=0)


def _rwkv_pre(xa, mods, p, n_ctx_tiles):
    b, t, d = xa.shape
    tm = RWKV_TILE
    n_tiles = t // tm
    nb = b
    n_ctx = n_ctx_tiles * (TOKEN_TILE // tm)
    mod_map = lambda bi, ti: (jnp.where(ti < n_ctx, nb, bi), 0, 0)
    gate_pad = -p['gate_g1'].shape[1] % LANES
    consts = [
        p['mu'],
        p['w_r'].astype(BF16), p['w_k'].astype(BF16), p['w_v'].astype(BF16),
        jnp.concatenate([p['decay_w1'][0], p['decay_w1'][1]], axis=1).astype(BF16),
        _block_diag2(p['decay_w2'][0], p['decay_w2'][1]).astype(BF16),
        p['decay_w0'].reshape(1, 2 * d),
        jnp.concatenate([p['aaa_a1'][0], p['aaa_a1'][1]], axis=1).astype(BF16),
        _block_diag2(p['aaa_a2'][0], p['aaa_a2'][1]).astype(BF16),
        p['aaa_a0'].reshape(1, 2 * d),
        jnp.pad(p['gate_g1'], ((0, 0), (0, gate_pad))).astype(BF16),
        jnp.pad(p['gate_g2'], ((0, gate_pad), (0, 0))).astype(BF16),
        p['k_k'][None], p['k_a'][None], p['r_k'].reshape(1, d),
        jnp.kron(jnp.eye(d // RWKV_HEAD, dtype=F32), jnp.ones((RWKV_HEAD, RWKV_HEAD), F32)).astype(BF16),
    ]
    tok = pl.BlockSpec((1, tm, d), lambda bi, ti: (bi, ti, 0))
    tok2 = pl.BlockSpec((2, 1, tm, d), lambda bi, ti: (0, bi, ti, 0))
    s1 = jax.ShapeDtypeStruct((b, t, d), F32)
    s2 = jax.ShapeDtypeStruct((2, b, t, d), F32)
    return pl.pallas_call(
        functools.partial(_rwkv_pre_kernel, n_ctx_tiles=n_ctx, n_tiles=n_tiles),
        grid=(b, n_tiles),
        in_specs=_halo_specs(tm, d, t // HALO) + [pl.BlockSpec((1, N_MOD, d), mod_map)]
        + [_const_spec(c.shape) for c in consts],
        out_specs=[tok, tok, tok, tok, tok2, tok2, tok, tok],
        out_shape=[s1, s1, s1, s1, s2, s2, s1, s1],
        compiler_params=_cparams(("parallel", "parallel")),
    )(xa, xa, xa, mods, *consts)


def _head_stack(x, lane_head):
    return jnp.concatenate([jnp.where(lane_head == hh, x, 0.0) for hh in range(HEADS_PER_GROUP)], axis=0)


def _wkv_kernel(r_ref, v_ref, k_ref, kk_ref, a_ref, ld_ref, ka_ref, y_ref, state_ref):
    rev = pl.program_id(1) == 1
    c = pl.program_id(2)
    gw = HEADS_PER_GROUP * RWKV_HEAD
    n_groups = r_ref.shape[2] // gw

    @pl.when(c == 0)
    def _():
        state_ref[...] = jnp.zeros(state_ref.shape, F32)

    row = lax.broadcasted_iota(jnp.int32, (CHUNK, gw), 0)
    lane = lax.broadcasted_iota(jnp.int32, (CHUNK, gw), 1)
    col = lane % CHUNK
    lane_head = lane // RWKV_HEAD
    incl = (row - col) * jnp.where(rev, -1, 1) >= 0
    strict = incl & (row != col)
    eye = (row == col).astype(F32)
    tri = jnp.where(incl[:, :CHUNK], 1.0, 0.0).astype(BF16)
    bd_r = lax.broadcasted_iota(jnp.int32, (gw, gw), 0) // RWKV_HEAD
    bd_c = lax.broadcasted_iota(jnp.int32, (gw, gw), 1) // RWKV_HEAD
    block_diag = bd_r == bd_c
    last = jnp.where(rev, 0, CHUNK - 1)
    ka = ka_ref[...]

    for gi in range(n_groups):
        sl = slice(gi * gw, (gi + 1) * gw)
        ld = ld_ref[0, 0, :, sl]
        ag = a_ref[0, 0, :, sl]
        kk = kk_ref[0, :, sl]
        kc = k_ref[0, :, sl] * (1.0 + (ag - 1.0) * ka[:, sl])
        vc = v_ref[0, :, sl]
        rc = r_ref[0, :, sl]
        ac = -kk
        bc = kk * ag
        h0 = state_ref[gi]

        log_p = _dot_split_lhs_rhs(tri, ld)
        log_p_last = jnp.sum(jnp.where(row == last, log_p, 0.0), axis=0, keepdims=True)
        dec_in = jnp.exp(log_p)
        dec_out = jnp.exp(-log_p)
        dec_tail = jnp.exp(log_p_last - log_p)
        a_t = ac * jnp.exp(log_p - ld)
        r_t = rc * dec_in
        ar = jnp.concatenate([a_t, r_t], axis=0)
        xb = _dot_nt(ar, _head_stack(bc * dec_out, lane_head))
        xk = _dot_nt(ar, _head_stack(kc * dec_out, lane_head))
        a_ab = jnp.where(strict, xb[:CHUNK], 0.0)
        a_rb = jnp.where(incl, xb[CHUNK:], 0.0)
        a_ak = jnp.where(strict, xk[:CHUNK], 0.0)
        a_rk = jnp.where(incl, xk[CHUNK:], 0.0)

        m = eye + a_ab
        a_pow = _dot(a_ab, _head_stack(a_ab, lane_head))
        n_sq = int(math.log2(CHUNK)) - 1
        for i in range(n_sq):
            stacked = _head_stack(a_pow, lane_head)
            if i < n_sq - 1:
                both = _dot(jnp.concatenate([a_pow, m], axis=0), stacked)
                a_pow, m = both[:CHUNK], m + both[CHUNK:]
            else:
                m = m + _dot(m, stacked)

        av = _dot(jnp.concatenate([a_ak, a_rk], axis=0), _head_stack(vc, lane_head))
        arh = _dot_nt(ar, h0)
        u = _dot(m, _head_stack(arh[:CHUNK] + av[:CHUNK], lane_head))
        y = arh[CHUNK:] + _dot(a_rb, _head_stack(u, lane_head)) + av[CHUNK:]
        y_ref[0, 0, :, sl] = y
        bk = jnp.concatenate([bc * dec_tail, kc * dec_tail], axis=0)
        upd = _dot_tn(jnp.concatenate([u, vc], axis=0), bk)
        state_ref[gi] = jnp.exp(log_p_last) * h0 + jnp.where(block_diag, upd, 0.0)


def _dot_split_lhs_rhs(tri_bf16, x):
    hi, lo = _split(x)
    return (jnp.dot(tri_bf16, hi, preferred_element_type=F32)
            + jnp.dot(tri_bf16, lo, preferred_element_type=F32))


def _wkv(r, v, k, kk, a, ld, k_a, ctx_len):
    b, t, d = r.shape
    n_chunks = t // CHUNK
    n_ctx = ctx_len // CHUNK

    def chunk_of(di, ci):
        back = jnp.where(ci < n_ctx, n_ctx - 1 - ci, n_chunks - 1 - (ci - n_ctx))
        return jnp.where(di == 0, ci, back)

    tok = pl.BlockSpec((1, CHUNK, d), lambda bi, di, ci: (bi, chunk_of(di, ci), 0))
    tok2 = pl.BlockSpec((1, 1, CHUNK, d), lambda bi, di, ci: (di, bi, chunk_of(di, ci), 0))
    gw = HEADS_PER_GROUP * RWKV_HEAD
    return pl.pallas_call(
        _wkv_kernel,
        grid=(b, 2, n_chunks),
        in_specs=[tok, tok, tok, tok, tok2, tok2, _const_spec((1, d))],
        out_specs=tok2,
        out_shape=jax.ShapeDtypeStruct((2, b, t, d), F32),
        scratch_shapes=[pltpu.VMEM((d // gw, gw, gw), F32)],
        compiler_params=_cparams(("arbitrary", "arbitrary", "arbitrary")),
    )(r, v, k, kk, a, ld, k_a[None])


def _rwkv_out_kernel(x_ref, y_ref, bv_ref, g_ref, mod_ref, gnw_ref, gnb_ref, gmean_ref, wo_ref, out_ref):
    y = y_ref[0, 0] + y_ref[1, 0]
    gmean = gmean_ref[...]
    dev = y - _dot_split_lhs(y, gmean)
    var = _dot_split_lhs(dev * dev, gmean)
    y_n = dev * lax.rsqrt(var + GN_EPS) * gnw_ref[...] + gnb_ref[...]
    z = (y_n + bv_ref[0]) * g_ref[0]
    out_ref[0] = x_ref[0] + mod_ref[0][2:3] * _dot(z, wo_ref[...])


def _rwkv_out(xa, y, bv, g, mods, gn_w, gn_b, w_o, n_ctx_tiles):
    b, t, d = xa.shape
    tm = TOKEN_TILE
    s = t - n_ctx_tiles * tm
    gmean = jnp.kron(jnp.eye(d // RWKV_HEAD, dtype=F32),
                     jnp.full((RWKV_HEAD, RWKV_HEAD), 1.0 / RWKV_HEAD, F32)).astype(BF16)
    lat = pl.BlockSpec((1, tm, d), lambda bi, ti: (bi, ti + n_ctx_tiles, 0))
    return pl.pallas_call(
        _rwkv_out_kernel,
        grid=(b, s // tm),
        in_specs=[
            lat,
            pl.BlockSpec((2, 1, tm, d), lambda bi, ti: (0, bi, ti + n_ctx_tiles, 0)),
            lat, lat,
            pl.BlockSpec((1, N_MOD, d), lambda bi, ti: (bi, 0, 0)),
            _const_spec((1, d)), _const_spec((1, d)), _const_spec((d, d)), _const_spec((d, d)),
        ],
        out_specs=pl.BlockSpec((1, tm, d), lambda bi, ti: (bi, ti, 0)),
        out_shape=jax.ShapeDtypeStruct((b, s, d), F32),
        compiler_params=_cparams(("parallel", "parallel")),
    )(xa, y, bv, g, mods, gn_w[None], gn_b[None], gmean, w_o.astype(BF16))


def _rope_tables(seq, ctx_len):
    rows = seq // GRID_W
    row = jnp.repeat(jnp.arange(rows, dtype=F32), GRID_W)
    col = jnp.tile(jnp.arange(GRID_W, dtype=F32), rows)
    n_freq = HEAD_DIM // 4
    inv_freq = ROPE_THETA ** (-jnp.arange(n_freq, dtype=F32) / n_freq)
    ang = jnp.concatenate([row[:, None] * inv_freq, col[:, None] * inv_freq], axis=-1)
    cos = jnp.repeat(jnp.cos(ang), 2, axis=1)
    sin = jnp.repeat(jnp.sin(ang), 2, axis=1) * jnp.tile(jnp.array([-1.0, 1.0], F32), HEAD_DIM // 2)
    cos = jnp.concatenate([jnp.ones((ctx_len, HEAD_DIM), F32), cos], axis=0)
    sin = jnp.concatenate([jnp.zeros((ctx_len, HEAD_DIM), F32), sin], axis=0)
    reps = LANES // HEAD_DIM
    return jnp.tile(cos, (1, reps)), jnp.tile(sin, (1, reps))


def kernel(x, c, ctx, c_ctx, ada_w, ada_b, attn_w_qkv, attn_q_gain, attn_k_gain, attn_w_o, rwkv_mu, rwkv_w_r, rwkv_w_k, rwkv_w_v, rwkv_w_o, rwkv_decay_w0, rwkv_decay_w1, rwkv_decay_w2, rwkv_aaa_a0, rwkv_aaa_a1, rwkv_aaa_a2, rwkv_gate_g1, rwkv_gate_g2, rwkv_k_k, rwkv_k_a, rwkv_r_k, rwkv_gn_w, rwkv_gn_b, ffn_w_up, ffn_conv_w, ffn_conv_b, ffn_w_down, final_gain):
    b, seq, d = x.shape
    ctx_len = ctx.shape[1]
    assert ada_w.shape[0] == 2 and attn_w_qkv.shape[0] == 1 and rwkv_w_r.shape[0] == 1
    assert b + 1 <= MOD_ROWS and ctx_len % TOKEN_TILE == 0 and seq % KV_TILE == 0 and seq % GRID_W == 0
    n_ctx_tiles = ctx_len // TOKEN_TILE

    cc = jnp.zeros((MOD_ROWS, d), F32).at[:b].set(c).at[b].set(c_ctx)
    mods = _ada_mods(cc, ada_w, ada_b)
    xa = jnp.concatenate([ctx, x], axis=1)

    cos_t, sin_t = _rope_tables(seq, ctx_len)
    q, kt, v = _attn_pre(xa, mods[0], attn_w_qkv[0], attn_q_gain[0], attn_k_gain[0], cos_t, sin_t, n_ctx_tiles)
    o = _flash(q, kt, v, ctx_len)
    xa = _attn_out(xa, o, mods[0], attn_w_o[0], n_ctx_tiles)
    xa = _conv_ffn(xa, mods[0], ffn_w_up[0], ffn_conv_w[0], ffn_conv_b[0], ffn_w_down[0], n_ctx_tiles, TOKEN_TILE)

    p = dict(mu=rwkv_mu[0], w_r=rwkv_w_r[0], w_k=rwkv_w_k[0], w_v=rwkv_w_v[0],
             decay_w0=rwkv_decay_w0[0], decay_w1=rwkv_decay_w1[0], decay_w2=rwkv_decay_w2[0],
             aaa_a0=rwkv_aaa_a0[0], aaa_a1=rwkv_aaa_a1[0], aaa_a2=rwkv_aaa_a2[0],
             gate_g1=rwkv_gate_g1[0], gate_g2=rwkv_gate_g2[0],
             k_k=rwkv_k_k[0], k_a=rwkv_k_a[0], r_k=rwkv_r_k[0])
    r, vv, k, kk, a, ld, g, bv = _rwkv_pre(xa, mods[1], p, n_ctx_tiles)
    y = _wkv(r, vv, k, kk, a, ld, rwkv_k_a[0], ctx_len)
    xl = _rwkv_out(xa, y, bv, g, mods[1], rwkv_gn_w[0], rwkv_gn_b[0], rwkv_w_o[0], n_ctx_tiles)
    return _conv_ffn(xl, mods[1], ffn_w_up[1], ffn_conv_w[1], ffn_conv_b[1], ffn_w_down[1], 0, TOKEN_TILE,
                     final_gain=final_gain)
```

```python
import functools
import math

import jax
import jax.numpy as jnp
from jax import lax
from jax.experimental import pallas as pl
from jax.experimental.pallas import tpu as pltpu

F32 = jnp.float32
BF16 = jnp.bfloat16

LANES = 128
SUBLANES = 8
VMEM_LIMIT_BYTES = 56 * 1024 * 1024

N_HEADS = 16
N_KV_HEADS = 4
HEAD_DIM = 64
GROUP = N_HEADS // N_KV_HEADS
RWKV_HEAD = 64
GRID_W = 64
ROPE_THETA = 10000.0
NORM_EPS = 1e-6
GN_EPS = 64e-5
N_MOD = 6
MOD_ROWS = 8

TOKEN_TILE = 256
RWKV_TILE = 128
Q_TILE = 256
KV_TILE = 2048
FF_CHUNK = 256
CHUNK = 64
HEADS_PER_GROUP = 4
HALO = SUBLANES


def _cparams(sem):
    return pltpu.CompilerParams(dimension_semantics=sem, vmem_limit_bytes=VMEM_LIMIT_BYTES)


def _dot(a, b):
    return jnp.dot(a.astype(BF16), b.astype(BF16), preferred_element_type=F32)


def _dot_nt(a, b):
    return lax.dot_general(a.astype(BF16), b.astype(BF16), (((1,), (1,)), ((), ())),
                           preferred_element_type=F32)


def _dot_tn(a, b):
    return lax.dot_general(a.astype(BF16), b.astype(BF16), (((0,), (0,)), ((), ())),
                           preferred_element_type=F32)


def _split(a):
    hi = a.astype(BF16)
    lo = (a - hi.astype(F32)).astype(BF16)
    return hi, lo


def _dot_split_lhs(a, b_bf16):
    hi, lo = _split(a)
    return (jnp.dot(hi, b_bf16, preferred_element_type=F32)
            + jnp.dot(lo, b_bf16, preferred_element_type=F32))


def _dot_split_both(a, b):
    ah, al = _split(a)
    bh, bl = _split(b)
    d = functools.partial(jnp.dot, preferred_element_type=F32)
    return d(ah, bh) + (d(ah, bl) + d(al, bh))


def _sigmoid(x):
    return 1.0 / (1.0 + jnp.exp(-x))


def _rms_norm(x):
    return x * lax.rsqrt(jnp.mean(x * x, axis=-1, keepdims=True) + NORM_EPS)


def _modulate(x, shift, scale):
    return _rms_norm(x) * (1.0 + scale) + shift


def _const_spec(shape):
    nd = len(shape)
    return pl.BlockSpec(shape, lambda *_: (0,) * nd)


def _ada_kernel(c_ref, w_ref, b_ref, o_ref):
    cc = c_ref[...]
    o_ref[0, 0] = _dot_split_both(cc * _sigmoid(cc), w_ref[0]) + b_ref[0, 0]


def _ada_mods(cc, ada_w, ada_b):
    depth, d, _ = ada_w.shape
    out = pl.pallas_call(
        _ada_kernel,
        grid=(depth, N_MOD),
        in_specs=[
            pl.BlockSpec((MOD_ROWS, d), lambda i, n: (0, 0)),
            pl.BlockSpec((1, d, d), lambda i, n: (i, 0, n)),
            pl.BlockSpec((1, 1, 1, d), lambda i, n: (i, n, 0, 0)),
        ],
        out_specs=pl.BlockSpec((1, 1, MOD_ROWS, d), lambda i, n: (i, n, 0, 0)),
        out_shape=jax.ShapeDtypeStruct((depth, N_MOD, MOD_ROWS, d), F32),
        compiler_params=_cparams(("arbitrary", "arbitrary")),
    )(cc, ada_w, ada_b.reshape(depth, N_MOD, 1, d))
    return out.transpose(0, 2, 1, 3)


def _rope(x, cos, sin):
    width = x.shape[1]
    reps = width // LANES
    cc = jnp.concatenate([cos] * reps, axis=1)
    ss = jnp.concatenate([sin] * reps, axis=1)
    lane = lax.broadcasted_iota(jnp.int32, x.shape, 1)
    partner = jnp.where((lane & 1) == 0, pltpu.roll(x, width - 1, 1), pltpu.roll(x, 1, 1))
    return x * cc + partner * ss


def _attn_pre_kernel(x_ref, mod_ref, w_ref, qg_ref, kg_ref, gq_ref, gk_ref, cos_ref, sin_ref,
                     q_ref, kt_ref, v_ref, *, q_dim, kv_dim):
    mod = mod_ref[0]
    h = _modulate(x_ref[0], mod[0:1], mod[1:2])
    qkv = _dot(h, w_ref[...])
    q = qkv[:, :q_dim]
    k = qkv[:, q_dim:q_dim + kv_dim]
    v = qkv[:, q_dim + kv_dim:]
    q = q * lax.rsqrt(_dot_split_lhs(q * q, gq_ref[...]) + NORM_EPS) * qg_ref[...]
    k = k * lax.rsqrt(_dot_split_lhs(k * k, gk_ref[...]) + NORM_EPS) * kg_ref[...]
    cos = cos_ref[...]
    sin = sin_ref[...]
    q = _rope(q, cos, sin) * (HEAD_DIM ** -0.5 * math.log2(math.e))
    k = _rope(k, cos, sin)
    q_ref[0] = q.astype(BF16)
    kt_ref[0] = k.T.astype(BF16)
    rows = v.shape[0]
    one_col = (lax.broadcasted_iota(jnp.int32, (rows, HEAD_DIM), 1) == 0).astype(F32)
    for g in range(N_KV_HEADS):
        v_ref[0, g] = jnp.concatenate([v[:, g * HEAD_DIM:(g + 1) * HEAD_DIM], one_col], axis=1).astype(BF16)


def _attn_pre(xa, mods, w_qkv, q_gain, k_gain, cos_t, sin_t, n_ctx_tiles):
    b, t, d = xa.shape
    q_dim = N_HEADS * HEAD_DIM
    kv_dim = N_KV_HEADS * HEAD_DIM
    tm = TOKEN_TILE
    nb = b
    gq = jnp.kron(jnp.eye(N_HEADS, dtype=F32), jnp.full((HEAD_DIM, HEAD_DIM), 1.0 / HEAD_DIM, F32)).astype(BF16)
    gk = jnp.kron(jnp.eye(N_KV_HEADS, dtype=F32), jnp.full((HEAD_DIM, HEAD_DIM), 1.0 / HEAD_DIM, F32)).astype(BF16)
    mod_map = lambda bi, ti: (jnp.where(ti < n_ctx_tiles, nb, bi), 0, 0)
    return pl.pallas_call(
        functools.partial(_attn_pre_kernel, q_dim=q_dim, kv_dim=kv_dim),
        grid=(b, t // tm),
        in_specs=[
            pl.BlockSpec((1, tm, d), lambda bi, ti: (bi, ti, 0)),
            pl.BlockSpec((1, N_MOD, d), mod_map),
            _const_spec((d, q_dim + 2 * kv_dim)),
            _const_spec((1, q_dim)),
            _const_spec((1, kv_dim)),
            _const_spec((q_dim, q_dim)),
            _const_spec((kv_dim, kv_dim)),
            pl.BlockSpec((tm, LANES), lambda bi, ti: (ti, 0)),
            pl.BlockSpec((tm, LANES), lambda bi, ti: (ti, 0)),
        ],
        out_specs=[
            pl.BlockSpec((1, tm, q_dim), lambda bi, ti: (bi, ti, 0)),
            pl.BlockSpec((1, kv_dim, tm), lambda bi, ti: (bi, 0, ti)),
            pl.BlockSpec((1, N_KV_HEADS, tm, LANES), lambda bi, ti: (bi, 0, ti, 0)),
        ],
        out_shape=[
            jax.ShapeDtypeStruct((b, t, q_dim), BF16),
            jax.ShapeDtypeStruct((b, kv_dim, t), BF16),
            jax.ShapeDtypeStruct((b, N_KV_HEADS, t, LANES), BF16),
        ],
        compiler_params=_cparams(("parallel", "parallel")),
    )(xa, mods, w_qkv.astype(BF16), jnp.tile(q_gain, N_HEADS)[None], jnp.tile(k_gain, N_KV_HEADS)[None],
      gq, gk, cos_t, sin_t)


def _flash_kernel(q_ref, kt_ref, v_ref, o_ref, q4_ref, m_ref, acc_ref, *, n_ctx_q, ctx_len, kv_tile, n_lat_kv):
    qi = pl.program_id(2)
    tq = q_ref.shape[1]
    q = q_ref[0]
    for h in range(GROUP):
        q4_ref[h * tq:(h + 1) * tq] = q[:, h * HEAD_DIM:(h + 1) * HEAD_DIM]
    m_ref[...] = jnp.full(m_ref.shape, -1e30, F32)
    acc_ref[...] = jnp.zeros(acc_ref.shape, F32)

    def visit(start, size):
        kt = kt_ref[0, :, pl.ds(start, size)]
        vt = v_ref[0, 0, pl.ds(start, size), :]
        for h in range(GROUP):
            rows = slice(h * tq, (h + 1) * tq)
            s = jnp.dot(q4_ref[rows], kt, preferred_element_type=F32)
            tiles = [s[:, t * LANES:(t + 1) * LANES] for t in range(size // LANES)]
            m_prev = m_ref[rows]
            m_new = jnp.maximum(m_prev, jnp.max(functools.reduce(jnp.maximum, tiles), axis=1, keepdims=True))
            p = jnp.concatenate([jnp.exp2(t - m_new) for t in tiles], axis=1).astype(BF16)
            acc_ref[rows] = jnp.exp2(m_prev - m_new) * acc_ref[rows] + jnp.dot(p, vt, preferred_element_type=F32)
            m_ref[rows] = m_new

    visit(0, ctx_len)

    @pl.when(qi >= n_ctx_q)
    def _():
        def body(j, carry):
            visit(pl.multiple_of(ctx_len + j * kv_tile, LANES), kv_tile)
            return carry
        lax.fori_loop(0, n_lat_kv, body, 0)

    acc = acc_ref[...]
    o = acc[:, :HEAD_DIM] / acc[:, HEAD_DIM:HEAD_DIM + 1]
    o_ref[0] = jnp.concatenate([o[h * tq:(h + 1) * tq] for h in range(GROUP)], axis=1).astype(BF16)


def _flash(q, kt, v, ctx_len):
    b, t, q_dim = q.shape
    tq = Q_TILE
    gw = GROUP * HEAD_DIM
    kv_tile = math.gcd(t - ctx_len, KV_TILE)
    kern = functools.partial(_flash_kernel, n_ctx_q=ctx_len // tq, ctx_len=ctx_len, kv_tile=kv_tile,
                             n_lat_kv=(t - ctx_len) // kv_tile)
    return pl.pallas_call(
        kern,
        grid=(b, N_KV_HEADS, t // tq),
        in_specs=[
            pl.BlockSpec((1, tq, gw), lambda bi, g, qi: (bi, qi, g)),
            pl.BlockSpec((1, HEAD_DIM, t), lambda bi, g, qi: (bi, g, 0)),
            pl.BlockSpec((1, 1, t, LANES), lambda bi, g, qi: (bi, g, 0, 0)),
        ],
        out_specs=pl.BlockSpec((1, tq, gw), lambda bi, g, qi: (bi, qi, g)),
        out_shape=jax.ShapeDtypeStruct((b, t, q_dim), BF16),
        scratch_shapes=[pltpu.VMEM((GROUP * tq, HEAD_DIM), BF16), pltpu.VMEM((GROUP * tq, LANES), F32),
                        pltpu.VMEM((GROUP * tq, LANES), F32)],
        compiler_params=_cparams(("parallel", "parallel", "arbitrary")),
    )(q, kt, v)


def _attn_out_kernel(x_ref, o_ref, mod_ref, w_ref, out_ref):
    y = jnp.dot(o_ref[0], w_ref[...], preferred_element_type=F32)
    out_ref[0] = x_ref[0] + mod_ref[0][2:3] * y


def _attn_out(xa, o, mods, w_o, n_ctx_tiles):
    b, t, d = xa.shape
    tm = TOKEN_TILE
    nb = b
    mod_map = lambda bi, ti: (jnp.where(ti < n_ctx_tiles, nb, bi), 0, 0)
    return pl.pallas_call(
        _attn_out_kernel,
        grid=(b, t // tm),
        in_specs=[
            pl.BlockSpec((1, tm, d), lambda bi, ti: (bi, ti, 0)),
            pl.BlockSpec((1, tm, o.shape[2]), lambda bi, ti: (bi, ti, 0)),
            pl.BlockSpec((1, N_MOD, d), mod_map),
            _const_spec(w_o.shape),
        ],
        out_specs=pl.BlockSpec((1, tm, d), lambda bi, ti: (bi, ti, 0)),
        out_shape=jax.ShapeDtypeStruct((b, t, d), F32),
        compiler_params=_cparams(("parallel", "parallel")),
    )(xa, o, mods, w_o.astype(BF16))


def _halo_specs(tm, d, n_row_blocks):
    per = tm // HALO
    return [
        pl.BlockSpec((1, tm, d), lambda bi, ti: (bi, ti, 0)),
        pl.BlockSpec((1, HALO, d), lambda bi, ti: (bi, jnp.maximum(ti * per - 1, 0), 0)),
        pl.BlockSpec((1, HALO, d), lambda bi, ti: (bi, jnp.minimum((ti + 1) * per, n_row_blocks - 1), 0)),
    ]


def _edge_keep(ti, n_ctx_tiles, n_tiles):
    first = (ti == 0) | (ti == n_ctx_tiles)
    last = (ti == n_tiles - 1) | (ti == n_ctx_tiles - 1)
    return jnp.where(first, 0.0, 1.0).astype(F32), jnp.where(last, 0.0, 1.0).astype(F32)


def _ffn_kernel(*refs, n_ctx_tiles, n_tiles, n_fc, final):
    if final:
        xm_ref, xp_ref, xn_ref, mod_ref, wup_ref, cw_ref, cb_ref, wdn_ref, fg_ref, out_ref, acc_ref = refs
    else:
        xm_ref, xp_ref, xn_ref, mod_ref, wup_ref, cw_ref, cb_ref, wdn_ref, out_ref, acc_ref = refs
    tm = xm_ref.shape[1]
    ext = tm + 2 * HALO
    mod = mod_ref[0]
    shift, scale, gate = mod[3:4], mod[4:5], mod[5:6]
    keep_prev, keep_next = _edge_keep(pl.program_id(1), n_ctx_tiles, n_tiles)
    xm = xm_ref[0]
    h = jnp.concatenate([_modulate(xp_ref[0], shift, scale) * keep_prev,
                         _modulate(xm, shift, scale),
                         _modulate(xn_ref[0], shift, scale) * keep_next], axis=0).astype(BF16)
    acc_ref[...] = jnp.zeros(acc_ref.shape, F32)

    def conv(u, cw, cb):
        y = cw[0:1] * pltpu.roll(u, 1, 0) + cw[1:2] * u + cw[2:3] * pltpu.roll(u, ext - 1, 0) + cb
        return y[HALO:HALO + tm]

    def body(j, carry):
        ug = jnp.dot(h, wup_ref[j], preferred_element_type=F32)
        uv = jnp.dot(h, wup_ref[n_fc + j], preferred_element_type=F32)
        cg = conv(ug, cw_ref[j], cb_ref[j])
        cv = conv(uv, cw_ref[n_fc + j], cb_ref[n_fc + j])
        act = cg * _sigmoid(cg) * cv
        acc_ref[...] += jnp.dot(act.astype(BF16), wdn_ref[j], preferred_element_type=F32)
        return carry

    lax.fori_loop(0, n_fc, body, 0)
    out = xm + gate * acc_ref[...]
    if final:
        out = _rms_norm(out) * fg_ref[...]
    out_ref[0] = out


def _conv_ffn(xa, mods, w_up, conv_w, conv_b, w_down, n_ctx_tiles, tm, final_gain=None):
    b, t, d = xa.shape
    d_ff = w_down.shape[0]
    n_fc = d_ff // FF_CHUNK
    n_tiles = t // tm
    nb = b
    final = final_gain is not None
    wup = w_up.astype(BF16).reshape(d, 2 * n_fc, FF_CHUNK).transpose(1, 0, 2)
    cw = conv_w.reshape(conv_w.shape[0], 2 * n_fc, FF_CHUNK).transpose(1, 0, 2)
    cb = conv_b.reshape(2 * n_fc, 1, FF_CHUNK)
    wdn = w_down.astype(BF16).reshape(n_fc, FF_CHUNK, d)
    mod_map = lambda bi, ti: (jnp.where(ti < n_ctx_tiles, nb, bi), 0, 0)
    in_specs = _halo_specs(tm, d, t // HALO) + [
        pl.BlockSpec((1, N_MOD, d), mod_map),
        _const_spec(wup.shape), _const_spec(cw.shape), _const_spec(cb.shape), _const_spec(wdn.shape),
    ]
    args = [xa, xa, xa, mods, wup, cw, cb, wdn]
    if final:
        in_specs.append(_const_spec((1, d)))
        args.append(final_gain[None])
    return pl.pallas_call(
        functools.partial(_ffn_kernel, n_ctx_tiles=n_ctx_tiles, n_tiles=n_tiles, n_fc=n_fc, final=final),
        grid=(b, n_tiles),
        in_specs=in_specs,
        out_specs=pl.BlockSpec((1, tm, d), lambda bi, ti: (bi, ti, 0)),
        out_shape=jax.ShapeDtypeStruct((b, t, d), F32),
        scratch_shapes=[pltpu.VMEM((tm, d), F32)],
        compiler_params=_cparams(("parallel", "parallel")),
    )(*args)


def _rwkv_pre_kernel(xm_ref, xp_ref, xn_ref, mod_ref, mu_ref, wr_ref, wk_ref, wv_ref,
                     dw1_ref, dw2_ref, w0_ref, a1_ref, a2_ref, a0_ref, g1_ref, g2_ref,
                     kk_ref, ka_ref, rk_ref, gsum_ref,
                     r_out, v_out, k_out, kk_out, a_out, ld_out, g_out, bv_out,
                     *, n_ctx_tiles, n_tiles):
    tm = xm_ref.shape[1]
    d = xm_ref.shape[2]
    ext = tm + 2 * HALO
    mod = mod_ref[0]
    shift, scale = mod[0:1], mod[1:2]
    keep_prev, keep_next = _edge_keep(pl.program_id(1), n_ctx_tiles, n_tiles)
    h = _modulate(xm_ref[0], shift, scale)
    he = jnp.concatenate([_modulate(xp_ref[0], shift, scale) * keep_prev, h,
                          _modulate(xn_ref[0], shift, scale) * keep_next], axis=0)
    xx = 0.5 * (pltpu.roll(he, 1, 0) + pltpu.roll(he, ext - 1, 0))[HALO:HALO + tm] - h
    mu = mu_ref[...]
    mix = lambda i: h + xx * mu[i:i + 1]
    r = _dot(mix(0), wr_ref[...])
    k = _dot(mix(2), wk_ref[...])
    v = _dot(mix(3), wv_ref[...])
    z = w0_ref[...] + _dot(jnp.tanh(_dot(mix(1), dw1_ref[...])), dw2_ref[...])
    log_decay = -math.exp(-0.5) * _sigmoid(z)
    a = _sigmoid(a0_ref[...] + _dot(_dot(mix(4), a1_ref[...]), a2_ref[...]))
    g = _dot(_sigmoid(_dot(mix(5), g1_ref[...])), g2_ref[...])
    gsum = gsum_ref[...]
    kx = k * kk_ref[...]
    kk = kx / jnp.maximum(jnp.sqrt(_dot_split_lhs(kx * kx, gsum)), 1e-12)
    ka = ka_ref[...]
    rk = r * rk_ref[...]
    bonus = rk * (k * (1.0 + (a[:, :d] - 1.0) * ka)) + rk * (k * (1.0 + (a[:, d:] - 1.0) * ka))
    r_out[0] = r
    v_out[0] = v
    k_out[0] = k
    kk_out[0] = kk
    a_out[0, 0] = a[:, :d]
    a_out[1, 0] = a[:, d:]
    ld_out[0, 0] = log_decay[:, :d]
    ld_out[1, 0] = log_decay[:, d:]
    g_out[0] = g
    bv_out[0] = _dot_split_lhs(bonus, gsum) * v


def _block_diag2(m0, m1):
    z = jnp.zeros_like(m0)
    return jnp.concatenate([jnp.concatenate([m0, z], axis=1), jnp.concatenate([z, m1], axis=1)], axis=0)


def _rwkv_pre(xa, mods, p, n_ctx_tiles):
    b, t, d = xa.shape
    tm = RWKV_TILE
    n_tiles = t // tm
    nb = b
    n_ctx = n_ctx_tiles * (TOKEN_TILE // tm)
    mod_map = lambda bi, ti: (jnp.where(ti < n_ctx, nb, bi), 0, 0)
    gate_pad = -p['gate_g1'].shape[1] % LANES
    consts = [
        p['mu'],
        p['w_r'].astype(BF16), p['w_k'].astype(BF16), p['w_v'].astype(BF16),
        jnp.concatenate([p['decay_w1'][0], p['decay_w1'][1]], axis=1).astype(BF16),
        _block_diag2(p['decay_w2'][0], p['decay_w2'][1]).astype(BF16),
        p['decay_w0'].reshape(1, 2 * d),
        jnp.concatenate([p['aaa_a1'][0], p['aaa_a1'][1]], axis=1).astype(BF16),
        _block_diag2(p['aaa_a2'][0], p['aaa_a2'][1]).astype(BF16),
        p['aaa_a0'].reshape(1, 2 * d),
        jnp.pad(p['gate_g1'], ((0, 0), (0, gate_pad))).astype(BF16),
        jnp.pad(p['gate_g2'], ((0, gate_pad), (0, 0))).astype(BF16),
        p['k_k'][None], p['k_a'][None], p['r_k'].reshape(1, d),
        jnp.kron(jnp.eye(d // RWKV_HEAD, dtype=F32), jnp.ones((RWKV_HEAD, RWKV_HEAD), F32)).astype(BF16),
    ]
    tok = pl.BlockSpec((1, tm, d), lambda bi, ti: (bi, ti, 0))
    tok2 = pl.BlockSpec((2, 1, tm, d), lambda bi, ti: (0, bi, ti, 0))
    s1 = jax.ShapeDtypeStruct((b, t, d), F32)
    s2 = jax.ShapeDtypeStruct((2, b, t, d), F32)
    return pl.pallas_call(
        functools.partial(_rwkv_pre_kernel, n_ctx_tiles=n_ctx, n_tiles=n_tiles),
        grid=(b, n_tiles),
        in_specs=_halo_specs(tm, d, t // HALO) + [pl.BlockSpec((1, N_MOD, d), mod_map)]
        + [_const_spec(c.shape) for c in consts],
        out_specs=[tok, tok, tok, tok, tok2, tok2, tok, tok],
        out_shape=[s1, s1, s1, s1, s2, s2, s1, s1],
        compiler_params=_cparams(("parallel", "parallel")),
    )(xa, xa, xa, mods, *consts)


def _head_stack(x, lane_head):
    return jnp.concatenate([jnp.where(lane_head == hh, x, 0.0) for hh in range(HEADS_PER_GROUP)], axis=0)


def _wkv_kernel(r_ref, v_ref, k_ref, kk_ref, a_ref, ld_ref, ka_ref, y_ref, state_ref):
    rev = pl.program_id(1) == 1
    c = pl.program_id(2)
    gw = HEADS_PER_GROUP * RWKV_HEAD
    n_groups = r_ref.shape[2] // gw

    @pl.when(c == 0)
    def _():
        state_ref[...] = jnp.zeros(state_ref.shape, F32)

    row = lax.broadcasted_iota(jnp.int32, (CHUNK, gw), 0)
    lane = lax.broadcasted_iota(jnp.int32, (CHUNK, gw), 1)
    col = lane % CHUNK
    lane_head = lane // RWKV_HEAD
    incl = (row - col) * jnp.where(rev, -1, 1) >= 0
    strict = incl & (row != col)
    eye = (row == col).astype(F32)
    tri = jnp.where(incl[:, :CHUNK], 1.0, 0.0).astype(BF16)
    bd_r = lax.broadcasted_iota(jnp.int32, (gw, gw), 0) // RWKV_HEAD
    bd_c = lax.broadcasted_iota(jnp.int32, (gw, gw), 1) // RWKV_HEAD
    block_diag = bd_r == bd_c
    last = jnp.where(rev, 0, CHUNK - 1)
    ka = ka_ref[...]

    groups = range(n_groups)
    cols = [slice(gi * gw, (gi + 1) * gw) for gi in groups]
    stack = lambda x: _head_stack(x, lane_head)
    ld = ld_ref[0, 0]
    ag = a_ref[0, 0]
    kk = kk_ref[0]
    kc = k_ref[0] * (1.0 + (ag - 1.0) * ka)
    vc = v_ref[0]
    bc = kk * ag
    h0 = [state_ref[gi] for gi in groups]

    log_p = _dot_split_lhs_rhs(tri, ld)
    log_p_last = jnp.sum(jnp.where(row[:, :1] == last, log_p, 0.0), axis=0, keepdims=True)
    dec_out = jnp.exp(-log_p)
    dec_tail = jnp.exp(log_p_last - log_p)
    ar = jnp.concatenate([-kk * jnp.exp(log_p - ld), r_ref[0] * jnp.exp(log_p)], axis=0)
    b_out = bc * dec_out
    k_out = kc * dec_out
    bk = jnp.concatenate([bc * dec_tail, kc * dec_tail], axis=0)
    state_decay = jnp.exp(log_p_last)

    xb = [_dot_nt(ar[:, c], stack(b_out[:, c])) for c in cols]
    xk = [_dot_nt(ar[:, c], stack(k_out[:, c])) for c in cols]
    a_ab = [jnp.where(strict, x[:CHUNK], 0.0) for x in xb]
    a_rb = [jnp.where(incl, x[CHUNK:], 0.0) for x in xb]
    a_kk = [jnp.concatenate([jnp.where(strict, x[:CHUNK], 0.0), jnp.where(incl, x[CHUNK:], 0.0)], axis=0) for x in xk]
    av = [_dot(a, stack(vc[:, c])) for a, c in zip(a_kk, cols)]
    arh = [_dot_nt(ar[:, c], h) for c, h in zip(cols, h0)]

    m = [eye + a for a in a_ab]
    a_pow = [_dot(a, stack(a)) for a in a_ab]
    n_sq = int(math.log2(CHUNK)) - 1
    for i in range(n_sq - 1):
        both = [_dot(jnp.concatenate([p, mm], axis=0), stack(p)) for p, mm in zip(a_pow, m)]
        a_pow = [x[:CHUNK] for x in both]
        m = [mm + x[CHUNK:] for mm, x in zip(m, both)]
    m = [mm + _dot(mm, stack(p)) for mm, p in zip(m, a_pow)]

    u = [_dot(mm, stack(h[:CHUNK] + x[:CHUNK])) for mm, h, x in zip(m, arh, av)]
    y = [h[CHUNK:] + _dot(a, stack(uu)) + x[CHUNK:] for h, a, uu, x in zip(arh, a_rb, u, av)]
    upd = [_dot_tn(jnp.concatenate([uu, vc[:, c]], axis=0), bk[:, c]) for uu, c in zip(u, cols)]
    y_ref[0, 0] = jnp.concatenate(y, axis=1)
    for gi in groups:
        state_ref[gi] = state_decay[:, cols[gi]] * h0[gi] + jnp.where(block_diag, upd[gi], 0.0)


def _dot_split_lhs_rhs(tri_bf16, x):
    hi, lo = _split(x)
    return (jnp.dot(tri_bf16, hi, preferred_element_type=F32)
            + jnp.dot(tri_bf16, lo, preferred_element_type=F32))


def _wkv(r, v, k, kk, a, ld, k_a, ctx_len):
    b, t, d = r.shape
    n_chunks = t // CHUNK
    n_ctx = ctx_len // CHUNK

    def chunk_of(di, ci):
        back = jnp.where(ci < n_ctx, n_ctx - 1 - ci, n_chunks - 1 - (ci - n_ctx))
        return jnp.where(di == 0, ci, back)

    tok = pl.BlockSpec((1, CHUNK, d), lambda bi, di, ci: (bi, chunk_of(di, ci), 0))
    tok2 = pl.BlockSpec((1, 1, CHUNK, d), lambda bi, di, ci: (di, bi, chunk_of(di, ci), 0))
    gw = HEADS_PER_GROUP * RWKV_HEAD
    return pl.pallas_call(
        _wkv_kernel,
        grid=(b, 2, n_chunks),
        in_specs=[tok, tok, tok, tok, tok2, tok2, _const_spec((1, d))],
        out_specs=tok2,
        out_shape=jax.ShapeDtypeStruct((2, b, t, d), F32),
        scratch_shapes=[pltpu.VMEM((d // gw, gw, gw), F32)],
        compiler_params=_cparams(("arbitrary", "arbitrary", "arbitrary")),
    )(r, v, k, kk, a, ld, k_a[None])


def _rwkv_out_kernel(x_ref, y_ref, bv_ref, g_ref, mod_ref, gnw_ref, gnb_ref, gmean_ref, wo_ref, out_ref):
    y = y_ref[0, 0] + y_ref[1, 0]
    gmean = gmean_ref[...]
    dev = y - _dot_split_lhs(y, gmean)
    var = _dot_split_lhs(dev * dev, gmean)
    y_n = dev * lax.rsqrt(var + GN_EPS) * gnw_ref[...] + gnb_ref[...]
    z = (y_n + bv_ref[0]) * g_ref[0]
    out_ref[0] = x_ref[0] + mod_ref[0][2:3] * _dot(z, wo_ref[...])


def _rwkv_out(xa, y, bv, g, mods, gn_w, gn_b, w_o, n_ctx_tiles):
    b, t, d = xa.shape
    tm = TOKEN_TILE
    s = t - n_ctx_tiles * tm
    gmean = jnp.kron(jnp.eye(d // RWKV_HEAD, dtype=F32),
                     jnp.full((RWKV_HEAD, RWKV_HEAD), 1.0 / RWKV_HEAD, F32)).astype(BF16)
    lat = pl.BlockSpec((1, tm, d), lambda bi, ti: (bi, ti + n_ctx_tiles, 0))
    return pl.pallas_call(
        _rwkv_out_kernel,
        grid=(b, s // tm),
        in_specs=[
            lat,
            pl.BlockSpec((2, 1, tm, d), lambda bi, ti: (0, bi, ti + n_ctx_tiles, 0)),
            lat, lat,
            pl.BlockSpec((1, N_MOD, d), lambda bi, ti: (bi, 0, 0)),
            _const_spec((1, d)), _const_spec((1, d)), _const_spec((d, d)), _const_spec((d, d)),
        ],
        out_specs=pl.BlockSpec((1, tm, d), lambda bi, ti: (bi, ti, 0)),
        out_shape=jax.ShapeDtypeStruct((b, s, d), F32),
        compiler_params=_cparams(("parallel", "parallel")),
    )(xa, y, bv, g, mods, gn_w[None], gn_b[None], gmean, w_o.astype(BF16))


def _rope_tables(seq, ctx_len):
    rows = seq // GRID_W
    row = jnp.repeat(jnp.arange(rows, dtype=F32), GRID_W)
    col = jnp.tile(jnp.arange(GRID_W, dtype=F32), rows)
    n_freq = HEAD_DIM // 4
    inv_freq = ROPE_THETA ** (-jnp.arange(n_freq, dtype=F32) / n_freq)
    ang = jnp.concatenate([row[:, None] * inv_freq, col[:, None] * inv_freq], axis=-1)
    cos = jnp.repeat(jnp.cos(ang), 2, axis=1)
    sin = jnp.repeat(jnp.sin(ang), 2, axis=1) * jnp.tile(jnp.array([-1.0, 1.0], F32), HEAD_DIM // 2)
    cos = jnp.concatenate([jnp.ones((ctx_len, HEAD_DIM), F32), cos], axis=0)
    sin = jnp.concatenate([jnp.zeros((ctx_len, HEAD_DIM), F32), sin], axis=0)
    reps = LANES // HEAD_DIM
    return jnp.tile(cos, (1, reps)), jnp.tile(sin, (1, reps))


def kernel(x, c, ctx, c_ctx, ada_w, ada_b, attn_w_qkv, attn_q_gain, attn_k_gain, attn_w_o, rwkv_mu, rwkv_w_r, rwkv_w_k, rwkv_w_v, rwkv_w_o, rwkv_decay_w0, rwkv_decay_w1, rwkv_decay_w2, rwkv_aaa_a0, rwkv_aaa_a1, rwkv_aaa_a2, rwkv_gate_g1, rwkv_gate_g2, rwkv_k_k, rwkv_k_a, rwkv_r_k, rwkv_gn_w, rwkv_gn_b, ffn_w_up, ffn_conv_w, ffn_conv_b, ffn_w_down, final_gain):
    b, seq, d = x.shape
    ctx_len = ctx.shape[1]
    assert ada_w.shape[0] == 2 and attn_w_qkv.shape[0] == 1 and rwkv_w_r.shape[0] == 1
    assert b + 1 <= MOD_ROWS and ctx_len % TOKEN_TILE == 0 and seq % TOKEN_TILE == 0 and seq % GRID_W == 0
    n_ctx_tiles = ctx_len // TOKEN_TILE

    cc = jnp.zeros((MOD_ROWS, d), F32).at[:b].set(c).at[b].set(c_ctx)
    mods = _ada_mods(cc, ada_w, ada_b)
    xa = jnp.concatenate([ctx, x], axis=1)

    cos_t, sin_t = _rope_tables(seq, ctx_len)
    q, kt, v = _attn_pre(xa, mods[0], attn_w_qkv[0], attn_q_gain[0], attn_k_gain[0], cos_t, sin_t, n_ctx_tiles)
    o = _flash(q, kt, v, ctx_len)
    xa = _attn_out(xa, o, mods[0], attn_w_o[0], n_ctx_tiles)
    xa = _conv_ffn(xa, mods[0], ffn_w_up[0], ffn_conv_w[0], ffn_conv_b[0], ffn_w_down[0], n_ctx_tiles, TOKEN_TILE)

    p = dict(mu=rwkv_mu[0], w_r=rwkv_w_r[0], w_k=rwkv_w_k[0], w_v=rwkv_w_v[0],
             decay_w0=rwkv_decay_w0[0], decay_w1=rwkv_decay_w1[0], decay_w2=rwkv_decay_w2[0],
             aaa_a0=rwkv_aaa_a0[0], aaa_a1=rwkv_aaa_a1[0], aaa_a2=rwkv_aaa_a2[0],
             gate_g1=rwkv_gate_g1[0], gate_g2=rwkv_gate_g2[0],
             k_k=rwkv_k_k[0], k_a=rwkv_k_a[0], r_k=rwkv_r_k[0])
    r, vv, k, kk, a, ld, g, bv = _rwkv_pre(xa, mods[1], p, n_ctx_tiles)
    y = _wkv(r, vv, k, kk, a, ld, rwkv_k_a[0], ctx_len)
    xl = _rwkv_out(xa, y, bv, g, mods[1], rwkv_gn_w[0], rwkv_gn_b[0], rwkv_w_o[0], n_ctx_tiles)
    return _conv_ffn(xl, mods[1], ffn_w_up[1], ffn_conv_w[1], ffn_conv_b[1], ffn_w_down[1], 0, TOKEN_TILE,
                     final_gain=final_gain)
```

```python
import functools
import math

import jax
import jax.numpy as jnp
from jax import lax
from jax.experimental import pallas as pl
from jax.experimental.pallas import tpu as pltpu

F32 = jnp.float32
BF16 = jnp.bfloat16

LANES = 128
SUBLANES = 8
VMEM_LIMIT_BYTES = 56 * 1024 * 1024

N_HEADS = 16
N_KV_HEADS = 4
HEAD_DIM = 64
GROUP = N_HEADS // N_KV_HEADS
RWKV_HEAD = 64
GRID_W = 64
ROPE_THETA = 10000.0
NORM_EPS = 1e-6
GN_EPS = 64e-5
N_MOD = 6
MOD_ROWS = 8

TOKEN_TILE = 256
RWKV_TILE = 128
Q_TILE = 256
KV_TILE = 2816
FF_CHUNK = 256
CHUNK = 64
HEADS_PER_GROUP = 4
HALO = SUBLANES


def _cparams(sem):
    return pltpu.CompilerParams(dimension_semantics=sem, vmem_limit_bytes=VMEM_LIMIT_BYTES)


def _dot(a, b):
    return jnp.dot(a.astype(BF16), b.astype(BF16), preferred_element_type=F32)


def _dot_nt(a, b):
    return lax.dot_general(a.astype(BF16), b.astype(BF16), (((1,), (1,)), ((), ())),
                           preferred_element_type=F32)


def _dot_tn(a, b):
    return lax.dot_general(a.astype(BF16), b.astype(BF16), (((0,), (0,)), ((), ())),
                           preferred_element_type=F32)


def _split(a):
    hi = a.astype(BF16)
    lo = (a - hi.astype(F32)).astype(BF16)
    return hi, lo


def _head_sum(a, block_diag_bf16):
    return jnp.dot(a.astype(BF16), block_diag_bf16, preferred_element_type=F32)


def _dot_split_both(a, b):
    ah, al = _split(a)
    bh, bl = _split(b)
    d = functools.partial(jnp.dot, preferred_element_type=F32)
    return d(ah, bh) + (d(ah, bl) + d(al, bh))


def _sigmoid(x):
    return 1.0 / (1.0 + jnp.exp(-x))


def _rms_norm(x):
    return x * lax.rsqrt(jnp.mean(x * x, axis=-1, keepdims=True) + NORM_EPS)


def _modulate(x, shift, scale):
    return _rms_norm(x) * (1.0 + scale) + shift


def _const_spec(shape):
    nd = len(shape)
    return pl.BlockSpec(shape, lambda *_: (0,) * nd)


def _ada_kernel(c_ref, w_ref, b_ref, o_ref):
    cc = c_ref[...]
    o_ref[0, 0] = _dot_split_both(cc * _sigmoid(cc), w_ref[0]) + b_ref[0, 0]


def _ada_mods(cc, ada_w, ada_b):
    depth, d, _ = ada_w.shape
    out = pl.pallas_call(
        _ada_kernel,
        grid=(depth, N_MOD),
        in_specs=[
            pl.BlockSpec((MOD_ROWS, d), lambda i, n: (0, 0)),
            pl.BlockSpec((1, d, d), lambda i, n: (i, 0, n)),
            pl.BlockSpec((1, 1, 1, d), lambda i, n: (i, n, 0, 0)),
        ],
        out_specs=pl.BlockSpec((1, 1, MOD_ROWS, d), lambda i, n: (i, n, 0, 0)),
        out_shape=jax.ShapeDtypeStruct((depth, N_MOD, MOD_ROWS, d), F32),
        compiler_params=_cparams(("arbitrary", "arbitrary")),
    )(cc, ada_w, ada_b.reshape(depth, N_MOD, 1, d))
    return out.transpose(0, 2, 1, 3)


def _rope(x, cos, sin):
    width = x.shape[1]
    reps = width // LANES
    cc = jnp.concatenate([cos] * reps, axis=1)
    ss = jnp.concatenate([sin] * reps, axis=1)
    lane = lax.broadcasted_iota(jnp.int32, x.shape, 1)
    partner = jnp.where((lane & 1) == 0, pltpu.roll(x, width - 1, 1), pltpu.roll(x, 1, 1))
    return x * cc + partner * ss


def _attn_pre_kernel(x_ref, mod_ref, w_ref, qg_ref, kg_ref, gq_ref, gk_ref, cos_ref, sin_ref,
                     q_ref, kt_ref, v_ref, *, q_dim, kv_dim):
    mod = mod_ref[0]
    h = _modulate(x_ref[0], mod[0:1], mod[1:2])
    qkv = _dot(h, w_ref[...])
    q = qkv[:, :q_dim]
    k = qkv[:, q_dim:q_dim + kv_dim]
    v = qkv[:, q_dim + kv_dim:]
    q = q * lax.rsqrt(_head_sum(q * q, gq_ref[...]) + NORM_EPS) * qg_ref[...]
    k = k * lax.rsqrt(_head_sum(k * k, gk_ref[...]) + NORM_EPS) * kg_ref[...]
    cos = cos_ref[...]
    sin = sin_ref[...]
    q = _rope(q, cos, sin) * (HEAD_DIM ** -0.5 * math.log2(math.e))
    k = _rope(k, cos, sin)
    q_ref[0] = q.astype(BF16)
    kt_ref[0] = k.T.astype(BF16)
    rows = v.shape[0]
    one_col = (lax.broadcasted_iota(jnp.int32, (rows, HEAD_DIM), 1) == 0).astype(F32)
    for g in range(N_KV_HEADS):
        v_ref[0, g] = jnp.concatenate([v[:, g * HEAD_DIM:(g + 1) * HEAD_DIM], one_col], axis=1).astype(BF16)


def _attn_pre(xa, mods, w_qkv, q_gain, k_gain, cos_t, sin_t, n_ctx_tiles):
    b, t, d = xa.shape
    q_dim = N_HEADS * HEAD_DIM
    kv_dim = N_KV_HEADS * HEAD_DIM
    tm = TOKEN_TILE
    nb = b
    gq = jnp.kron(jnp.eye(N_HEADS, dtype=F32), jnp.full((HEAD_DIM, HEAD_DIM), 1.0 / HEAD_DIM, F32)).astype(BF16)
    gk = jnp.kron(jnp.eye(N_KV_HEADS, dtype=F32), jnp.full((HEAD_DIM, HEAD_DIM), 1.0 / HEAD_DIM, F32)).astype(BF16)
    mod_map = lambda bi, ti: (jnp.where(ti < n_ctx_tiles, nb, bi), 0, 0)
    return pl.pallas_call(
        functools.partial(_attn_pre_kernel, q_dim=q_dim, kv_dim=kv_dim),
        grid=(b, t // tm),
        in_specs=[
            pl.BlockSpec((1, tm, d), lambda bi, ti: (bi, ti, 0)),
            pl.BlockSpec((1, N_MOD, d), mod_map),
            _const_spec((d, q_dim + 2 * kv_dim)),
            _const_spec((1, q_dim)),
            _const_spec((1, kv_dim)),
            _const_spec((q_dim, q_dim)),
            _const_spec((kv_dim, kv_dim)),
            pl.BlockSpec((tm, LANES), lambda bi, ti: (ti, 0)),
            pl.BlockSpec((tm, LANES), lambda bi, ti: (ti, 0)),
        ],
        out_specs=[
            pl.BlockSpec((1, tm, q_dim), lambda bi, ti: (bi, ti, 0)),
            pl.BlockSpec((1, kv_dim, tm), lambda bi, ti: (bi, 0, ti)),
            pl.BlockSpec((1, N_KV_HEADS, tm, LANES), lambda bi, ti: (bi, 0, ti, 0)),
        ],
        out_shape=[
            jax.ShapeDtypeStruct((b, t, q_dim), BF16),
            jax.ShapeDtypeStruct((b, kv_dim, t), BF16),
            jax.ShapeDtypeStruct((b, N_KV_HEADS, t, LANES), BF16),
        ],
        compiler_params=_cparams(("parallel", "parallel")),
    )(xa, mods, w_qkv.astype(BF16), jnp.tile(q_gain, N_HEADS)[None], jnp.tile(k_gain, N_KV_HEADS)[None],
      gq, gk, cos_t, sin_t)


def _flash_kernel(q_ref, kt_ref, v_ref, o_ref, q4_ref, *, n_ctx_q, ctx_len, kv_tiles):
    qi = pl.program_id(2)
    tq = q_ref.shape[1]
    q = q_ref[0]
    for h in range(GROUP):
        q4_ref[h * tq:(h + 1) * tq] = q[:, h * HEAD_DIM:(h + 1) * HEAD_DIM]

    def attend(tiles):
        steps = [(start, size, h) for start, size in tiles for h in range(GROUP)]
        scores = lambda st: jnp.dot(q4_ref[st[2] * tq:(st[2] + 1) * tq], kt_ref[0, :, st[0]:st[0] + st[1]],
                                    preferred_element_type=F32)
        m = [None] * GROUP
        acc = [None] * GROUP
        s_next = scores(steps[0])
        for i, (start, size, h) in enumerate(steps):
            s = s_next
            if i + 1 < len(steps):
                s_next = scores(steps[i + 1])
            cols = [s[:, t * LANES:(t + 1) * LANES] for t in range(size // LANES)]
            m_cur = jnp.max(functools.reduce(jnp.maximum, cols), axis=1, keepdims=True)
            m_new = jnp.broadcast_to(m_cur, (tq, LANES)) if m[h] is None else jnp.maximum(m[h], m_cur)
            p = jnp.concatenate([jnp.exp2(c - m_new) for c in cols], axis=1).astype(BF16)
            pv = jnp.dot(p, v_ref[0, 0, start:start + size, :], preferred_element_type=F32)
            acc[h] = pv if acc[h] is None else jnp.exp2(m[h] - m_new) * acc[h] + pv
            m[h] = m_new
        o = [a[:, :HEAD_DIM] / a[:, HEAD_DIM:HEAD_DIM + 1] for a in acc]
        o_ref[0] = jnp.concatenate(o, axis=1).astype(BF16)

    @pl.when(qi < n_ctx_q)
    def _():
        attend([(0, ctx_len)])

    @pl.when(qi >= n_ctx_q)
    def _():
        attend(kv_tiles)


def _flash(q, kt, v, ctx_len):
    b, t, q_dim = q.shape
    tq = Q_TILE
    gw = GROUP * HEAD_DIM
    n_kv = next(n for n in range(1, t // LANES + 1) if t % (n * LANES) == 0 and t // n <= KV_TILE)
    kv_tiles = [(i * (t // n_kv), t // n_kv) for i in range(n_kv)]
    kern = functools.partial(_flash_kernel, n_ctx_q=ctx_len // tq, ctx_len=ctx_len, kv_tiles=kv_tiles)
    return pl.pallas_call(
        kern,
        grid=(b, N_KV_HEADS, t // tq),
        in_specs=[
            pl.BlockSpec((1, tq, gw), lambda bi, g, qi: (bi, qi, g)),
            pl.BlockSpec((1, HEAD_DIM, t), lambda bi, g, qi: (bi, g, 0)),
            pl.BlockSpec((1, 1, t, LANES), lambda bi, g, qi: (bi, g, 0, 0)),
        ],
        out_specs=pl.BlockSpec((1, tq, gw), lambda bi, g, qi: (bi, qi, g)),
        out_shape=jax.ShapeDtypeStruct((b, t, q_dim), BF16),
        scratch_shapes=[pltpu.VMEM((GROUP * tq, HEAD_DIM), BF16)],
        compiler_params=_cparams(("parallel", "parallel", "arbitrary")),
    )(q, kt, v)


def _attn_out_kernel(x_ref, o_ref, mod_ref, w_ref, out_ref):
    y = jnp.dot(o_ref[0], w_ref[...], preferred_element_type=F32)
    out_ref[0] = x_ref[0] + mod_ref[0][2:3] * y


def _attn_out(xa, o, mods, w_o, n_ctx_tiles):
    b, t, d = xa.shape
    tm = TOKEN_TILE
    nb = b
    mod_map = lambda bi, ti: (jnp.where(ti < n_ctx_tiles, nb, bi), 0, 0)
    return pl.pallas_call(
        _attn_out_kernel,
        grid=(b, t // tm),
        in_specs=[
            pl.BlockSpec((1, tm, d), lambda bi, ti: (bi, ti, 0)),
            pl.BlockSpec((1, tm, o.shape[2]), lambda bi, ti: (bi, ti, 0)),
            pl.BlockSpec((1, N_MOD, d), mod_map),
            _const_spec(w_o.shape),
        ],
        out_specs=pl.BlockSpec((1, tm, d), lambda bi, ti: (bi, ti, 0)),
        out_shape=jax.ShapeDtypeStruct((b, t, d), F32),
        compiler_params=_cparams(("parallel", "parallel")),
    )(xa, o, mods, w_o.astype(BF16))


def _halo_specs(tm, d, n_row_blocks):
    per = tm // HALO
    return [
        pl.BlockSpec((1, tm, d), lambda bi, ti: (bi, ti, 0)),
        pl.BlockSpec((1, HALO, d), lambda bi, ti: (bi, jnp.maximum(ti * per - 1, 0), 0)),
        pl.BlockSpec((1, HALO, d), lambda bi, ti: (bi, jnp.minimum((ti + 1) * per, n_row_blocks - 1), 0)),
    ]


def _edge_keep(ti, n_ctx_tiles, n_tiles):
    first = (ti == 0) | (ti == n_ctx_tiles)
    last = (ti == n_tiles - 1) | (ti == n_ctx_tiles - 1)
    return jnp.where(first, 0.0, 1.0).astype(F32), jnp.where(last, 0.0, 1.0).astype(F32)


def _ffn_kernel(*refs, n_ctx_tiles, n_tiles, n_fc, final):
    if final:
        xm_ref, xp_ref, xn_ref, mod_ref, wup_ref, cw_ref, cb_ref, wdn_ref, fg_ref, out_ref = refs
    else:
        xm_ref, xp_ref, xn_ref, mod_ref, wup_ref, cw_ref, cb_ref, wdn_ref, out_ref = refs
    tm = xm_ref.shape[1]
    ext = tm + 2 * HALO
    mod = mod_ref[0]
    shift, scale, gate = mod[3:4], mod[4:5], mod[5:6]
    keep_prev, keep_next = _edge_keep(pl.program_id(1), n_ctx_tiles, n_tiles)
    xm = xm_ref[0]
    h = jnp.concatenate([_modulate(xp_ref[0], shift, scale) * keep_prev,
                         _modulate(xm, shift, scale),
                         _modulate(xn_ref[0], shift, scale) * keep_next], axis=0).astype(BF16)

    def conv(u, cw, cb):
        y = cw[0:1] * pltpu.roll(u, 1, 0) + cw[1:2] * u + cw[2:3] * pltpu.roll(u, ext - 1, 0) + cb
        return y[HALO:HALO + tm]

    up = lambda j: (jnp.dot(h, wup_ref[j], preferred_element_type=F32),
                    jnp.dot(h, wup_ref[n_fc + j], preferred_element_type=F32))
    acc = None
    nxt = up(0)
    for j in range(n_fc):
        ug, uv = nxt
        if j + 1 < n_fc:
            nxt = up(j + 1)
        cg = conv(ug, cw_ref[j], cb_ref[j])
        cv = conv(uv, cw_ref[n_fc + j], cb_ref[n_fc + j])
        act = cg * _sigmoid(cg) * cv
        down = jnp.dot(act.astype(BF16), wdn_ref[j], preferred_element_type=F32)
        acc = down if acc is None else acc + down

    out = xm + gate * acc
    if final:
        out = _rms_norm(out) * fg_ref[...]
    out_ref[0] = out


def _conv_ffn(xa, mods, w_up, conv_w, conv_b, w_down, n_ctx_tiles, tm, final_gain=None):
    b, t, d = xa.shape
    d_ff = w_down.shape[0]
    n_fc = d_ff // FF_CHUNK
    n_tiles = t // tm
    nb = b
    final = final_gain is not None
    wup = w_up.astype(BF16).reshape(d, 2 * n_fc, FF_CHUNK).transpose(1, 0, 2)
    cw = conv_w.reshape(conv_w.shape[0], 2 * n_fc, FF_CHUNK).transpose(1, 0, 2)
    cb = conv_b.reshape(2 * n_fc, 1, FF_CHUNK)
    wdn = w_down.astype(BF16).reshape(n_fc, FF_CHUNK, d)
    mod_map = lambda bi, ti: (jnp.where(ti < n_ctx_tiles, nb, bi), 0, 0)
    in_specs = _halo_specs(tm, d, t // HALO) + [
        pl.BlockSpec((1, N_MOD, d), mod_map),
        _const_spec(wup.shape), _const_spec(cw.shape), _const_spec(cb.shape), _const_spec(wdn.shape),
    ]
    args = [xa, xa, xa, mods, wup, cw, cb, wdn]
    if final:
        in_specs.append(_const_spec((1, d)))
        args.append(final_gain[None])
    return pl.pallas_call(
        functools.partial(_ffn_kernel, n_ctx_tiles=n_ctx_tiles, n_tiles=n_tiles, n_fc=n_fc, final=final),
        grid=(b, n_tiles),
        in_specs=in_specs,
        out_specs=pl.BlockSpec((1, tm, d), lambda bi, ti: (bi, ti, 0)),
        out_shape=jax.ShapeDtypeStruct((b, t, d), F32),
        compiler_params=_cparams(("parallel", "parallel")),
    )(*args)


def _rwkv_pre_kernel(xm_ref, xp_ref, xn_ref, mod_ref, mu_ref, wr_ref, wk_ref, wv_ref,
                     dw1_ref, dw2_ref, w0_ref, a1_ref, a2_ref, a0_ref, g1_ref, g2_ref,
                     kk_ref, ka_ref, rk_ref, gsum_ref,
                     r_out, v_out, k_out, kk_out, a_out, ld_out, g_out, bv_out,
                     *, n_ctx_tiles, n_tiles):
    tm = xm_ref.shape[1]
    d = xm_ref.shape[2]
    ext = tm + 2 * HALO
    mod = mod_ref[0]
    shift, scale = mod[0:1], mod[1:2]
    keep_prev, keep_next = _edge_keep(pl.program_id(1), n_ctx_tiles, n_tiles)
    h = _modulate(xm_ref[0], shift, scale)
    he = jnp.concatenate([_modulate(xp_ref[0], shift, scale) * keep_prev, h,
                          _modulate(xn_ref[0], shift, scale) * keep_next], axis=0)
    xx = 0.5 * (pltpu.roll(he, 1, 0) + pltpu.roll(he, ext - 1, 0))[HALO:HALO + tm] - h
    mu = mu_ref[...]
    mix = lambda i: h + xx * mu[i:i + 1]
    r = _dot(mix(0), wr_ref[...])
    k = _dot(mix(2), wk_ref[...])
    v = _dot(mix(3), wv_ref[...])
    z = w0_ref[...] + _dot(jnp.tanh(_dot(mix(1), dw1_ref[...])), dw2_ref[...])
    log_decay = -math.exp(-0.5) * _sigmoid(z)
    a = _sigmoid(a0_ref[...] + _dot(_dot(mix(4), a1_ref[...]), a2_ref[...]))
    g = _dot(_sigmoid(_dot(mix(5), g1_ref[...])), g2_ref[...])
    gsum = gsum_ref[...]
    kx = k * kk_ref[...]
    kk = kx / jnp.maximum(jnp.sqrt(_head_sum(kx * kx, gsum)), 1e-12)
    ka = ka_ref[...]
    rk = r * rk_ref[...]
    bonus = rk * (k * (1.0 + (a[:, :d] - 1.0) * ka)) + rk * (k * (1.0 + (a[:, d:] - 1.0) * ka))
    r_out[0] = r
    v_out[0] = v
    k_out[0] = k
    kk_out[0] = kk
    a_out[0, 0] = a[:, :d]
    a_out[1, 0] = a[:, d:]
    ld_out[0, 0] = log_decay[:, :d]
    ld_out[1, 0] = log_decay[:, d:]
    g_out[0] = g
    bv_out[0] = _head_sum(bonus, gsum) * v


def _block_diag2(m0, m1):
    z = jnp.zeros_like(m0)
    return jnp.concatenate([jnp.concatenate([m0, z], axis=1), jnp.concatenate([z, m1], axis=1)], axis=0)


def _rwkv_pre(xa, mods, p, n_ctx_tiles):
    b, t, d = xa.shape
    tm = RWKV_TILE
    n_tiles = t // tm
    nb = b
    n_ctx = n_ctx_tiles * (TOKEN_TILE // tm)
    mod_map = lambda bi, ti: (jnp.where(ti < n_ctx, nb, bi), 0, 0)
    gate_pad = -p['gate_g1'].shape[1] % LANES
    consts = [
        p['mu'],
        p['w_r'].astype(BF16), p['w_k'].astype(BF16), p['w_v'].astype(BF16),
        jnp.concatenate([p['decay_w1'][0], p['decay_w1'][1]], axis=1).astype(BF16),
        _block_diag2(p['decay_w2'][0], p['decay_w2'][1]).astype(BF16),
        p['decay_w0'].reshape(1, 2 * d),
        jnp.concatenate([p['aaa_a1'][0], p['aaa_a1'][1]], axis=1).astype(BF16),
        _block_diag2(p['aaa_a2'][0], p['aaa_a2'][1]).astype(BF16),
        p['aaa_a0'].reshape(1, 2 * d),
        jnp.pad(p['gate_g1'], ((0, 0), (0, gate_pad))).astype(BF16),
        jnp.pad(p['gate_g2'], ((0, gate_pad), (0, 0))).astype(BF16),
        p['k_k'][None], p['k_a'][None], p['r_k'].reshape(1, d),
        jnp.kron(jnp.eye(d // RWKV_HEAD, dtype=F32), jnp.ones((RWKV_HEAD, RWKV_HEAD), F32)).astype(BF16),
    ]
    tok = pl.BlockSpec((1, tm, d), lambda bi, ti: (bi, ti, 0))
    tok2 = pl.BlockSpec((2, 1, tm, d), lambda bi, ti: (0, bi, ti, 0))
    s1 = jax.ShapeDtypeStruct((b, t, d), F32)
    s2 = jax.ShapeDtypeStruct((2, b, t, d), F32)
    return pl.pallas_call(
        functools.partial(_rwkv_pre_kernel, n_ctx_tiles=n_ctx, n_tiles=n_tiles),
        grid=(b, n_tiles),
        in_specs=_halo_specs(tm, d, t // HALO) + [pl.BlockSpec((1, N_MOD, d), mod_map)]
        + [_const_spec(c.shape) for c in consts],
        out_specs=[tok, tok, tok, tok, tok2, tok2, tok, tok],
        out_shape=[s1, s1, s1, s1, s2, s2, s1, s1],
        compiler_params=_cparams(("parallel", "parallel")),
    )(xa, xa, xa, mods, *consts)


def _head_stack(x, lane_head):
    return jnp.concatenate([jnp.where(lane_head == hh, x, 0.0) for hh in range(HEADS_PER_GROUP)], axis=0)


def _wkv_kernel(rf_ref, vf_ref, kf_ref, kkf_ref, af_ref, ldf_ref, rb_ref, vb_ref, kb_ref, kkb_ref, ab_ref, ldb_ref,
                ka_ref, yf_ref, yb_ref, state_ref):
    gw = HEADS_PER_GROUP * RWKV_HEAD
    n_groups = rf_ref.shape[2] // gw

    @pl.when(pl.program_id(1) == 0)
    def _():
        state_ref[...] = jnp.zeros(state_ref.shape, F32)

    row = lax.broadcasted_iota(jnp.int32, (CHUNK, gw), 0)
    lane = lax.broadcasted_iota(jnp.int32, (CHUNK, gw), 1)
    col = lane % CHUNK
    lane_head = lane // RWKV_HEAD
    eye = (row == col).astype(F32)
    bd_r = lax.broadcasted_iota(jnp.int32, (gw, gw), 0) // RWKV_HEAD
    bd_c = lax.broadcasted_iota(jnp.int32, (gw, gw), 1) // RWKV_HEAD
    block_diag = bd_r == bd_c
    ka = ka_ref[...]
    stack = lambda x: _head_stack(x, lane_head)
    cols = [slice(gi * gw, (gi + 1) * gw) for gi in range(n_groups)]

    wide = []
    for refs, incl, last in (((rf_ref, vf_ref, kf_ref, kkf_ref, af_ref, ldf_ref), row >= col, CHUNK - 1),
                             ((rb_ref, vb_ref, kb_ref, kkb_ref, ab_ref, ldb_ref), row <= col, 0)):
        r_ref, v_ref, k_ref, kk_ref, a_ref, ld_ref = refs
        ld = ld_ref[0, 0]
        ag = a_ref[0, 0]
        kk = kk_ref[0]
        kc = k_ref[0] * (1.0 + (ag - 1.0) * ka)
        bc = kk * ag
        tri = jnp.where(incl[:, :CHUNK], 1.0, 0.0).astype(BF16)
        log_p = _dot_split_lhs_rhs(tri, ld)
        log_p_last = log_p[last:last + 1]
        dec_out = jnp.exp(-log_p)
        dec_tail = jnp.exp(log_p_last - log_p)
        wide.append(dict(
            incl=incl, strict=incl & (row != col), vc=v_ref[0],
            ar=jnp.concatenate([-kk * jnp.exp(log_p - ld), r_ref[0] * jnp.exp(log_p)], axis=0),
            b_out=bc * dec_out, k_out=kc * dec_out,
            bk=jnp.concatenate([bc * dec_tail, kc * dec_tail], axis=0),
            state_decay=jnp.exp(log_p_last)))

    chains = [(di, gi) for di in range(2) for gi in range(n_groups)]
    w = lambda ch, name: wide[ch[0]][name]
    part = lambda ch, name: wide[ch[0]][name][:, cols[ch[1]]]
    h0 = [state_ref[di * n_groups + gi] for di, gi in chains]
    xb = [_dot_nt(part(ch, 'ar'), stack(part(ch, 'b_out'))) for ch in chains]
    xk = [_dot_nt(part(ch, 'ar'), stack(part(ch, 'k_out'))) for ch in chains]
    a_ab = [jnp.where(w(ch, 'strict'), x[:CHUNK], 0.0) for ch, x in zip(chains, xb)]
    a_rb = [jnp.where(w(ch, 'incl'), x[CHUNK:], 0.0) for ch, x in zip(chains, xb)]
    a_kk = [jnp.concatenate([jnp.where(w(ch, 'strict'), x[:CHUNK], 0.0), jnp.where(w(ch, 'incl'), x[CHUNK:], 0.0)],
                            axis=0) for ch, x in zip(chains, xk)]
    av = [_dot(a, stack(part(ch, 'vc'))) for ch, a in zip(chains, a_kk)]
    arh = [_dot_nt(part(ch, 'ar'), h) for ch, h in zip(chains, h0)]

    m = [eye + a for a in a_ab]
    a_pow = [_dot(a, stack(a)) for a in a_ab]
    n_sq = int(math.log2(CHUNK)) - 1
    for i in range(n_sq - 1):
        both = [_dot(jnp.concatenate([p, mm], axis=0), stack(p)) for p, mm in zip(a_pow, m)]
        a_pow = [x[:CHUNK] for x in both]
        m = [mm + x[CHUNK:] for mm, x in zip(m, both)]
    m = [mm + _dot(mm, stack(p)) for mm, p in zip(m, a_pow)]

    u = [_dot(mm, stack(h[:CHUNK] + x[:CHUNK])) for mm, h, x in zip(m, arh, av)]
    y = [h[CHUNK:] + _dot(a, stack(uu)) + x[CHUNK:] for h, a, uu, x in zip(arh, a_rb, u, av)]
    upd = [_dot_tn(jnp.concatenate([uu, part(ch, 'vc')], axis=0), part(ch, 'bk')) for ch, uu in zip(chains, u)]
    yf_ref[0] = jnp.concatenate(y[:n_groups], axis=1)
    yb_ref[0] = jnp.concatenate(y[n_groups:], axis=1)
    for i, ch in enumerate(chains):
        state_ref[i] = part(ch, 'state_decay') * h0[i] + jnp.where(block_diag, upd[i], 0.0)


def _dot_split_lhs_rhs(tri_bf16, x):
    hi, lo = _split(x)
    return (jnp.dot(tri_bf16, hi, preferred_element_type=F32)
            + jnp.dot(tri_bf16, lo, preferred_element_type=F32))


def _wkv(r, v, k, kk, a, ld, k_a, ctx_len):
    b, t, d = r.shape
    n_chunks = t // CHUNK
    n_ctx = ctx_len // CHUNK

    def back(ci):
        return jnp.where(ci < n_ctx, n_ctx - 1 - ci, n_chunks - 1 - (ci - n_ctx))

    fwd = pl.BlockSpec((1, CHUNK, d), lambda bi, ci: (bi, ci, 0))
    bwd = pl.BlockSpec((1, CHUNK, d), lambda bi, ci: (bi, back(ci), 0))
    fwd2 = pl.BlockSpec((1, 1, CHUNK, d), lambda bi, ci: (0, bi, ci, 0))
    bwd2 = pl.BlockSpec((1, 1, CHUNK, d), lambda bi, ci: (1, bi, back(ci), 0))
    gw = HEADS_PER_GROUP * RWKV_HEAD
    y_shape = jax.ShapeDtypeStruct((b, t, d), F32)
    return pl.pallas_call(
        _wkv_kernel,
        grid=(b, n_chunks),
        in_specs=[fwd, fwd, fwd, fwd, fwd2, fwd2, bwd, bwd, bwd, bwd, bwd2, bwd2, _const_spec((1, d))],
        out_specs=[fwd, bwd],
        out_shape=[y_shape, y_shape],
        scratch_shapes=[pltpu.VMEM((2 * d // gw, gw, gw), F32)],
        compiler_params=_cparams(("arbitrary", "arbitrary")),
    )(r, v, k, kk, a, ld, r, v, k, kk, a, ld, k_a[None])


def _rwkv_out_kernel(x_ref, yf_ref, yb_ref, bv_ref, g_ref, mod_ref, gnw_ref, gnb_ref, gmean_ref, wo_ref, out_ref):
    y = yf_ref[0] + yb_ref[0]
    gmean = gmean_ref[...]
    dev = y - _head_sum(y, gmean)
    var = _head_sum(dev * dev, gmean)
    y_n = dev * lax.rsqrt(var + GN_EPS) * gnw_ref[...] + gnb_ref[...]
    z = (y_n + bv_ref[0]) * g_ref[0]
    out_ref[0] = x_ref[0] + mod_ref[0][2:3] * _dot(z, wo_ref[...])


def _rwkv_out(xa, y, bv, g, mods, gn_w, gn_b, w_o, n_ctx_tiles):
    b, t, d = xa.shape
    tm = TOKEN_TILE
    s = t - n_ctx_tiles * tm
    gmean = jnp.kron(jnp.eye(d // RWKV_HEAD, dtype=F32),
                     jnp.full((RWKV_HEAD, RWKV_HEAD), 1.0 / RWKV_HEAD, F32)).astype(BF16)
    lat = pl.BlockSpec((1, tm, d), lambda bi, ti: (bi, ti + n_ctx_tiles, 0))
    return pl.pallas_call(
        _rwkv_out_kernel,
        grid=(b, s // tm),
        in_specs=[
            lat, lat, lat, lat, lat,
            pl.BlockSpec((1, N_MOD, d), lambda bi, ti: (bi, 0, 0)),
            _const_spec((1, d)), _const_spec((1, d)), _const_spec((d, d)), _const_spec((d, d)),
        ],
        out_specs=pl.BlockSpec((1, tm, d), lambda bi, ti: (bi, ti, 0)),
        out_shape=jax.ShapeDtypeStruct((b, s, d), F32),
        compiler_params=_cparams(("parallel", "parallel")),
    )(xa, y[0], y[1], bv, g, mods, gn_w[None], gn_b[None], gmean, w_o.astype(BF16))


def _rope_tables(seq, ctx_len):
    rows = seq // GRID_W
    row = jnp.repeat(jnp.arange(rows, dtype=F32), GRID_W)
    col = jnp.tile(jnp.arange(GRID_W, dtype=F32), rows)
    n_freq = HEAD_DIM // 4
    inv_freq = ROPE_THETA ** (-jnp.arange(n_freq, dtype=F32) / n_freq)
    ang = jnp.concatenate([row[:, None] * inv_freq, col[:, None] * inv_freq], axis=-1)
    cos = jnp.repeat(jnp.cos(ang), 2, axis=1)
    sin = jnp.repeat(jnp.sin(ang), 2, axis=1) * jnp.tile(jnp.array([-1.0, 1.0], F32), HEAD_DIM // 2)
    cos = jnp.concatenate([jnp.ones((ctx_len, HEAD_DIM), F32), cos], axis=0)
    sin = jnp.concatenate([jnp.zeros((ctx_len, HEAD_DIM), F32), sin], axis=0)
    reps = LANES // HEAD_DIM
    return jnp.tile(cos, (1, reps)), jnp.tile(sin, (1, reps))


def kernel(x, c, ctx, c_ctx, ada_w, ada_b, attn_w_qkv, attn_q_gain, attn_k_gain, attn_w_o, rwkv_mu, rwkv_w_r, rwkv_w_k, rwkv_w_v, rwkv_w_o, rwkv_decay_w0, rwkv_decay_w1, rwkv_decay_w2, rwkv_aaa_a0, rwkv_aaa_a1, rwkv_aaa_a2, rwkv_gate_g1, rwkv_gate_g2, rwkv_k_k, rwkv_k_a, rwkv_r_k, rwkv_gn_w, rwkv_gn_b, ffn_w_up, ffn_conv_w, ffn_conv_b, ffn_w_down, final_gain):
    b, seq, d = x.shape
    ctx_len = ctx.shape[1]
    assert ada_w.shape[0] == 2 and attn_w_qkv.shape[0] == 1 and rwkv_w_r.shape[0] == 1
    assert b + 1 <= MOD_ROWS and ctx_len % TOKEN_TILE == 0 and seq % TOKEN_TILE == 0 and seq % GRID_W == 0
    n_ctx_tiles = ctx_len // TOKEN_TILE

    cc = jnp.zeros((MOD_ROWS, d), F32).at[:b].set(c).at[b].set(c_ctx)
    mods = _ada_mods(cc, ada_w, ada_b)
    xa = jnp.concatenate([ctx, x], axis=1)

    cos_t, sin_t = _rope_tables(seq, ctx_len)
    q, kt, v = _attn_pre(xa, mods[0], attn_w_qkv[0], attn_q_gain[0], attn_k_gain[0], cos_t, sin_t, n_ctx_tiles)
    o = _flash(q, kt, v, ctx_len)
    xa = _attn_out(xa, o, mods[0], attn_w_o[0], n_ctx_tiles)
    xa = _conv_ffn(xa, mods[0], ffn_w_up[0], ffn_conv_w[0], ffn_conv_b[0], ffn_w_down[0], n_ctx_tiles, TOKEN_TILE)

    p = dict(mu=rwkv_mu[0], w_r=rwkv_w_r[0], w_k=rwkv_w_k[0], w_v=rwkv_w_v[0],
             decay_w0=rwkv_decay_w0[0], decay_w1=rwkv_decay_w1[0], decay_w2=rwkv_decay_w2[0],
             aaa_a0=rwkv_aaa_a0[0], aaa_a1=rwkv_aaa_a1[0], aaa_a2=rwkv_aaa_a2[0],
             gate_g1=rwkv_gate_g1[0], gate_g2=rwkv_gate_g2[0],
             k_k=rwkv_k_k[0], k_a=rwkv_k_a[0], r_k=rwkv_r_k[0])
    r, vv, k, kk, a, ld, g, bv = _rwkv_pre(xa, mods[1], p, n_ctx_tiles)
    y = _wkv(r, vv, k, kk, a, ld, rwkv_k_a[0], ctx_len)
    xl = _rwkv_out(xa, y, bv, g, mods[1], rwkv_gn_w[0], rwkv_gn_b[0], rwkv_w_o[0], n_ctx_tiles)
    return _conv_ffn(xl, mods[1], ffn_w_up[1], ffn_conv_w[1], ffn_conv_b[1], ffn_w_down[1], 0, TOKEN_TILE,
                     final_gain=final_gain)
```

```python
import functools
import math

import jax
import jax.numpy as jnp
from jax import lax
from jax.experimental import pallas as pl
from jax.experimental.pallas import tpu as pltpu

F32 = jnp.float32
BF16 = jnp.bfloat16

LANES = 128
SUBLANES = 8
MXU_WIDTH = 256
VMEM_LIMIT_BYTES = 56 * 1024 * 1024

N_HEADS = 16
N_KV_HEADS = 4
HEAD_DIM = 64
GROUP = N_HEADS // N_KV_HEADS
RWKV_HEAD = 64
GRID_W = 64
ROPE_THETA = 10000.0
NORM_EPS = 1e-6
GN_EPS = 64e-5
N_MOD = 6
MOD_ROWS = 8

TOKEN_TILE = 256
RWKV_TILE = 256
Q_TILE = 256
KV_TILE = 2816
KV_HEADS_PER_STEP = 1
FF_CHUNK = 256
CHUNK = 64
HEADS_PER_GROUP = 4
WKV_STEP_CHUNKS = 2
HALO = SUBLANES


def _cparams(sem):
    return pltpu.CompilerParams(dimension_semantics=sem, vmem_limit_bytes=VMEM_LIMIT_BYTES)


def _dot(a, b):
    return jnp.dot(a.astype(BF16), b.astype(BF16), preferred_element_type=F32)


def _dot_nt(a, b):
    return lax.dot_general(a.astype(BF16), b.astype(BF16), (((1,), (1,)), ((), ())),
                           preferred_element_type=F32)


def _dot_tn(a, b):
    return lax.dot_general(a.astype(BF16), b.astype(BF16), (((0,), (0,)), ((), ())),
                           preferred_element_type=F32)


def _split(a):
    hi = a.astype(BF16)
    lo = (a - hi.astype(F32)).astype(BF16)
    return hi, lo


def _head_indicator(head_dim, scale):
    return jnp.kron(jnp.eye(MXU_WIDTH // head_dim, dtype=F32), jnp.full((head_dim, head_dim), scale, F32)).astype(BF16)


def _head_sum(a, indicator):
    a = a.astype(BF16)
    blocks = [jnp.dot(a[:, j:j + MXU_WIDTH], indicator, preferred_element_type=F32)
              for j in range(0, a.shape[1], MXU_WIDTH)]
    return blocks[0] if len(blocks) == 1 else jnp.concatenate(blocks, axis=1)


def _dot_split_both(a, b):
    ah, al = _split(a)
    bh, bl = _split(b)
    d = functools.partial(jnp.dot, preferred_element_type=F32)
    return d(ah, bh) + (d(ah, bl) + d(al, bh))


def _sigmoid(x):
    return 1.0 / (1.0 + jnp.exp(-x))


def _rms_norm(x):
    return x * lax.rsqrt(jnp.mean(x * x, axis=-1, keepdims=True) + NORM_EPS)


def _modulate(x, shift, scale):
    return _rms_norm(x) * (1.0 + scale) + shift


def _const_spec(shape):
    nd = len(shape)
    return pl.BlockSpec(shape, lambda *_: (0,) * nd)


def _ada_kernel(c_ref, w_ref, b_ref, o_ref):
    cc = c_ref[...]
    o_ref[0, 0] = _dot_split_both(cc * _sigmoid(cc), w_ref[0]) + b_ref[0, 0]


def _ada_mods(cc, ada_w, ada_b):
    depth, d, _ = ada_w.shape
    out = pl.pallas_call(
        _ada_kernel,
        grid=(depth, N_MOD),
        in_specs=[
            pl.BlockSpec((MOD_ROWS, d), lambda i, n: (0, 0)),
            pl.BlockSpec((1, d, d), lambda i, n: (i, 0, n)),
            pl.BlockSpec((1, 1, 1, d), lambda i, n: (i, n, 0, 0)),
        ],
        out_specs=pl.BlockSpec((1, 1, MOD_ROWS, d), lambda i, n: (i, n, 0, 0)),
        out_shape=jax.ShapeDtypeStruct((depth, N_MOD, MOD_ROWS, d), F32),
        compiler_params=_cparams(("arbitrary", "arbitrary")),
    )(cc, ada_w, ada_b.reshape(depth, N_MOD, 1, d))
    return out.transpose(0, 2, 1, 3)


def _rope(x, cos, sin):
    width = x.shape[1]
    reps = width // LANES
    cc = jnp.concatenate([cos] * reps, axis=1)
    ss = jnp.concatenate([sin] * reps, axis=1)
    lane = lax.broadcasted_iota(jnp.int32, x.shape, 1)
    partner = jnp.where((lane & 1) == 0, pltpu.roll(x, width - 1, 1), pltpu.roll(x, 1, 1))
    return x * cc + partner * ss


def _attn_pre_kernel(x_ref, mod_ref, w_ref, qg_ref, kg_ref, hmean_ref, cos_ref, sin_ref,
                     q_ref, kt_ref, v_ref, *, q_dim, kv_dim):
    mod = mod_ref[0]
    h = _modulate(x_ref[0], mod[0:1], mod[1:2])
    qkv = _dot(h, w_ref[...])
    q = qkv[:, :q_dim]
    k = qkv[:, q_dim:q_dim + kv_dim]
    v = qkv[:, q_dim + kv_dim:]
    hmean = hmean_ref[...]
    q = q * lax.rsqrt(_head_sum(q * q, hmean) + NORM_EPS) * qg_ref[...]
    k = k * lax.rsqrt(_head_sum(k * k, hmean) + NORM_EPS) * kg_ref[...]
    cos = cos_ref[...]
    sin = sin_ref[...]
    q = _rope(q, cos, sin) * (HEAD_DIM ** -0.5 * math.log2(math.e))
    k = _rope(k, cos, sin)
    q_ref[0] = q.astype(BF16)
    kt_ref[0] = k.T.astype(BF16)
    rows = v.shape[0]
    one_col = (lax.broadcasted_iota(jnp.int32, (rows, HEAD_DIM), 1) == 0).astype(F32)
    for g in range(N_KV_HEADS):
        v_ref[0, g] = jnp.concatenate([v[:, g * HEAD_DIM:(g + 1) * HEAD_DIM], one_col], axis=1).astype(BF16)


def _attn_pre(xa, mods, w_qkv, q_gain, k_gain, cos_t, sin_t, n_ctx_tiles):
    b, t, d = xa.shape
    q_dim = N_HEADS * HEAD_DIM
    kv_dim = N_KV_HEADS * HEAD_DIM
    tm = TOKEN_TILE
    nb = b
    mod_map = lambda bi, ti: (jnp.where(ti < n_ctx_tiles, nb, bi), 0, 0)
    return pl.pallas_call(
        functools.partial(_attn_pre_kernel, q_dim=q_dim, kv_dim=kv_dim),
        grid=(b, t // tm),
        in_specs=[
            pl.BlockSpec((1, tm, d), lambda bi, ti: (bi, ti, 0)),
            pl.BlockSpec((1, N_MOD, d), mod_map),
            _const_spec((d, q_dim + 2 * kv_dim)),
            _const_spec((1, q_dim)),
            _const_spec((1, kv_dim)),
            _const_spec((MXU_WIDTH, MXU_WIDTH)),
            pl.BlockSpec((tm, LANES), lambda bi, ti: (ti, 0)),
            pl.BlockSpec((tm, LANES), lambda bi, ti: (ti, 0)),
        ],
        out_specs=[
            pl.BlockSpec((1, tm, q_dim), lambda bi, ti: (bi, ti, 0)),
            pl.BlockSpec((1, kv_dim, tm), lambda bi, ti: (bi, 0, ti)),
            pl.BlockSpec((1, N_KV_HEADS, tm, LANES), lambda bi, ti: (bi, 0, ti, 0)),
        ],
        out_shape=[
            jax.ShapeDtypeStruct((b, t, q_dim), BF16),
            jax.ShapeDtypeStruct((b, kv_dim, t), BF16),
            jax.ShapeDtypeStruct((b, N_KV_HEADS, t, LANES), BF16),
        ],
        compiler_params=_cparams(("parallel", "parallel")),
    )(xa, mods, w_qkv.astype(BF16), jnp.tile(q_gain, N_HEADS)[None], jnp.tile(k_gain, N_KV_HEADS)[None],
      _head_indicator(HEAD_DIM, 1.0 / HEAD_DIM), cos_t, sin_t)


def _flash_kernel(q_ref, kt_ref, v_ref, o_ref, q4_ref, *, n_ctx_q, ctx_len, kv_tiles):
    qi = pl.program_id(2)
    tq = q_ref.shape[1]
    n_heads = q_ref.shape[2] // HEAD_DIM
    q = q_ref[0]
    for h in range(n_heads):
        q4_ref[h * tq:(h + 1) * tq] = q[:, h * HEAD_DIM:(h + 1) * HEAD_DIM]

    def attend(tiles):
        steps = [(start, size, g * GROUP + j) for g in range(n_heads // GROUP) for start, size in tiles
                 for j in range(GROUP)]

        def scores(st):
            start, size, h = st
            g = h // GROUP
            return jnp.dot(q4_ref[h * tq:(h + 1) * tq], kt_ref[0, g * HEAD_DIM:(g + 1) * HEAD_DIM, start:start + size],
                           preferred_element_type=F32)

        m = [None] * n_heads
        acc = [None] * n_heads
        s_next = scores(steps[0])
        for i, (start, size, h) in enumerate(steps):
            s = s_next
            if i + 1 < len(steps):
                s_next = scores(steps[i + 1])
            cols = [s[:, t * LANES:(t + 1) * LANES] for t in range(size // LANES)]
            m_cur = jnp.max(functools.reduce(jnp.maximum, cols), axis=1, keepdims=True)
            m_new = jnp.broadcast_to(m_cur, (tq, LANES)) if m[h] is None else jnp.maximum(m[h], m_cur)
            p = jnp.concatenate([jnp.exp2(c - m_new) for c in cols], axis=1).astype(BF16)
            pv = jnp.dot(p, v_ref[0, h // GROUP, start:start + size, :], preferred_element_type=F32)
            acc[h] = pv if acc[h] is None else jnp.exp2(m[h] - m_new) * acc[h] + pv
            m[h] = m_new
        o = [a[:, :HEAD_DIM] / a[:, HEAD_DIM:HEAD_DIM + 1] for a in acc]
        o_ref[0] = jnp.concatenate(o, axis=1).astype(BF16)

    @pl.when(qi < n_ctx_q)
    def _():
        attend([(0, ctx_len)])

    @pl.when(qi >= n_ctx_q)
    def _():
        attend(kv_tiles)


def _flash(q, kt, v, ctx_len):
    b, t, q_dim = q.shape
    tq = Q_TILE
    kvh = KV_HEADS_PER_STEP
    gw = kvh * GROUP * HEAD_DIM
    n_kv = next(n for n in range(1, t // LANES + 1) if t % (n * LANES) == 0 and t // n <= KV_TILE)
    kv_tiles = [(i * (t // n_kv), t // n_kv) for i in range(n_kv)]
    kern = functools.partial(_flash_kernel, n_ctx_q=ctx_len // tq, ctx_len=ctx_len, kv_tiles=kv_tiles)
    return pl.pallas_call(
        kern,
        grid=(b, N_KV_HEADS // kvh, t // tq),
        in_specs=[
            pl.BlockSpec((1, tq, gw), lambda bi, g, qi: (bi, qi, g)),
            pl.BlockSpec((1, kvh * HEAD_DIM, t), lambda bi, g, qi: (bi, g, 0)),
            pl.BlockSpec((1, kvh, t, LANES), lambda bi, g, qi: (bi, g, 0, 0)),
        ],
        out_specs=pl.BlockSpec((1, tq, gw), lambda bi, g, qi: (bi, qi, g)),
        out_shape=jax.ShapeDtypeStruct((b, t, q_dim), BF16),
        scratch_shapes=[pltpu.VMEM((kvh * GROUP * tq, HEAD_DIM), BF16)],
        compiler_params=_cparams(("parallel", "parallel", "arbitrary")),
    )(q, kt, v)


def _attn_out_kernel(x_ref, o_ref, mod_ref, w_ref, out_ref):
    y = jnp.dot(o_ref[0], w_ref[...], preferred_element_type=F32)
    out_ref[0] = x_ref[0] + mod_ref[0][2:3] * y


def _attn_out(xa, o, mods, w_o, n_ctx_tiles):
    b, t, d = xa.shape
    tm = TOKEN_TILE
    nb = b
    mod_map = lambda bi, ti: (jnp.where(ti < n_ctx_tiles, nb, bi), 0, 0)
    return pl.pallas_call(
        _attn_out_kernel,
        grid=(b, t // tm),
        in_specs=[
            pl.BlockSpec((1, tm, d), lambda bi, ti: (bi, ti, 0)),
            pl.BlockSpec((1, tm, o.shape[2]), lambda bi, ti: (bi, ti, 0)),
            pl.BlockSpec((1, N_MOD, d), mod_map),
            _const_spec(w_o.shape),
        ],
        out_specs=pl.BlockSpec((1, tm, d), lambda bi, ti: (bi, ti, 0)),
        out_shape=jax.ShapeDtypeStruct((b, t, d), F32),
        compiler_params=_cparams(("parallel", "parallel")),
    )(xa, o, mods, w_o.astype(BF16))


def _halo_specs(tm, d, n_row_blocks):
    per = tm // HALO
    return [
        pl.BlockSpec((1, tm, d), lambda bi, ti: (bi, ti, 0)),
        pl.BlockSpec((1, HALO, d), lambda bi, ti: (bi, jnp.maximum(ti * per - 1, 0), 0)),
        pl.BlockSpec((1, HALO, d), lambda bi, ti: (bi, jnp.minimum((ti + 1) * per, n_row_blocks - 1), 0)),
    ]


def _edge_keep(ti, n_ctx_tiles, n_tiles):
    first = (ti == 0) | (ti == n_ctx_tiles)
    last = (ti == n_tiles - 1) | (ti == n_ctx_tiles - 1)
    return jnp.where(first, 0.0, 1.0).astype(F32), jnp.where(last, 0.0, 1.0).astype(F32)


def _ffn_kernel(*refs, n_ctx_tiles, n_tiles, n_fc, final):
    if final:
        xm_ref, xp_ref, xn_ref, mod_ref, wup_ref, cw_ref, cb_ref, wdn_ref, fg_ref, out_ref = refs
    else:
        xm_ref, xp_ref, xn_ref, mod_ref, wup_ref, cw_ref, cb_ref, wdn_ref, out_ref = refs
    tm = xm_ref.shape[1]
    ext = tm + 2 * HALO
    mod = mod_ref[0]
    shift, scale, gate = mod[3:4], mod[4:5], mod[5:6]
    keep_prev, keep_next = _edge_keep(pl.program_id(1), n_ctx_tiles, n_tiles)
    xm = xm_ref[0]
    h = jnp.concatenate([_modulate(xp_ref[0], shift, scale) * keep_prev,
                         _modulate(xm, shift, scale),
                         _modulate(xn_ref[0], shift, scale) * keep_next], axis=0).astype(BF16)

    def conv(u, cw, cb):
        y = cw[0:1] * pltpu.roll(u, 1, 0) + cw[1:2] * u + cw[2:3] * pltpu.roll(u, ext - 1, 0) + cb
        return y[HALO:HALO + tm]

    up = lambda j: (jnp.dot(h, wup_ref[j], preferred_element_type=F32),
                    jnp.dot(h, wup_ref[n_fc + j], preferred_element_type=F32))
    acc = None
    nxt = up(0)
    for j in range(n_fc):
        ug, uv = nxt
        if j + 1 < n_fc:
            nxt = up(j + 1)
        cg = conv(ug, cw_ref[j], cb_ref[j])
        cv = conv(uv, cw_ref[n_fc + j], cb_ref[n_fc + j])
        act = cg * _sigmoid(cg) * cv
        down = jnp.dot(act.astype(BF16), wdn_ref[j], preferred_element_type=F32)
        acc = down if acc is None else acc + down

    out = xm + gate * acc
    if final:
        out = _rms_norm(out) * fg_ref[...]
    out_ref[0] = out


def _conv_ffn(xa, mods, w_up, conv_w, conv_b, w_down, n_ctx_tiles, tm, final_gain=None):
    b, t, d = xa.shape
    d_ff = w_down.shape[0]
    n_fc = d_ff // FF_CHUNK
    n_tiles = t // tm
    nb = b
    final = final_gain is not None
    wup = w_up.astype(BF16).reshape(d, 2 * n_fc, FF_CHUNK).transpose(1, 0, 2)
    cw = conv_w.reshape(conv_w.shape[0], 2 * n_fc, FF_CHUNK).transpose(1, 0, 2)
    cb = conv_b.reshape(2 * n_fc, 1, FF_CHUNK)
    wdn = w_down.astype(BF16).reshape(n_fc, FF_CHUNK, d)
    mod_map = lambda bi, ti: (jnp.where(ti < n_ctx_tiles, nb, bi), 0, 0)
    in_specs = _halo_specs(tm, d, t // HALO) + [
        pl.BlockSpec((1, N_MOD, d), mod_map),
        _const_spec(wup.shape), _const_spec(cw.shape), _const_spec(cb.shape), _const_spec(wdn.shape),
    ]
    args = [xa, xa, xa, mods, wup, cw, cb, wdn]
    if final:
        in_specs.append(_const_spec((1, d)))
        args.append(final_gain[None])
    return pl.pallas_call(
        functools.partial(_ffn_kernel, n_ctx_tiles=n_ctx_tiles, n_tiles=n_tiles, n_fc=n_fc, final=final),
        grid=(b, n_tiles),
        in_specs=in_specs,
        out_specs=pl.BlockSpec((1, tm, d), lambda bi, ti: (bi, ti, 0)),
        out_shape=jax.ShapeDtypeStruct((b, t, d), F32),
        compiler_params=_cparams(("parallel", "parallel")),
    )(*args)


def _rwkv_pre_kernel(xm_ref, xp_ref, xn_ref, mod_ref, mu_ref, wr_ref, wk_ref, wv_ref,
                     dw1_ref, dw2_ref, w0_ref, a1_ref, a2_ref, a0_ref, g1_ref, g2_ref,
                     kk_ref, ka_ref, rk_ref, gsum_ref,
                     r_out, v_out, k_out, kk_out, a_out, ld_out, g_out, bv_out,
                     *, n_ctx_tiles, n_tiles):
    tm = xm_ref.shape[1]
    d = xm_ref.shape[2]
    ext = tm + 2 * HALO
    mod = mod_ref[0]
    shift, scale = mod[0:1], mod[1:2]
    keep_prev, keep_next = _edge_keep(pl.program_id(1), n_ctx_tiles, n_tiles)
    h = _modulate(xm_ref[0], shift, scale)
    he = jnp.concatenate([_modulate(xp_ref[0], shift, scale) * keep_prev, h,
                          _modulate(xn_ref[0], shift, scale) * keep_next], axis=0)
    xx = 0.5 * (pltpu.roll(he, 1, 0) + pltpu.roll(he, ext - 1, 0))[HALO:HALO + tm] - h
    mu = mu_ref[...]
    mix = lambda i: h + xx * mu[i:i + 1]
    r = _dot(mix(0), wr_ref[...])
    k = _dot(mix(2), wk_ref[...])
    v = _dot(mix(3), wv_ref[...])
    z = w0_ref[...] + _dot(jnp.tanh(_dot(mix(1), dw1_ref[...])), dw2_ref[...])
    log_decay = -math.exp(-0.5) * _sigmoid(z)
    a = _sigmoid(a0_ref[...] + _dot(_dot(mix(4), a1_ref[...]), a2_ref[...]))
    g = _dot(_sigmoid(_dot(mix(5), g1_ref[...])), g2_ref[...])
    gsum = gsum_ref[...]
    kx = k * kk_ref[...]
    kk = kx / jnp.maximum(jnp.sqrt(_head_sum(kx * kx, gsum)), 1e-12)
    ka = ka_ref[...]
    bonus = (r * rk_ref[...]) * k * (2.0 + (a[:, :d] + a[:, d:] - 2.0) * ka)
    r_out[0] = r
    v_out[0] = v
    k_out[0] = k
    kk_out[0] = kk
    a_out[0, 0] = a[:, :d]
    a_out[1, 0] = a[:, d:]
    ld_out[0, 0] = log_decay[:, :d]
    ld_out[1, 0] = log_decay[:, d:]
    g_out[0] = g
    bv_out[0] = _head_sum(bonus, gsum) * v


def _block_diag2(m0, m1):
    z = jnp.zeros_like(m0)
    return jnp.concatenate([jnp.concatenate([m0, z], axis=1), jnp.concatenate([z, m1], axis=1)], axis=0)


def _rwkv_pre(xa, mods, p, n_ctx_tiles):
    b, t, d = xa.shape
    tm = RWKV_TILE
    n_tiles = t // tm
    nb = b
    n_ctx = n_ctx_tiles * (TOKEN_TILE // tm)
    mod_map = lambda bi, ti: (jnp.where(ti < n_ctx, nb, bi), 0, 0)
    gate_pad = -p['gate_g1'].shape[1] % LANES
    consts = [
        p['mu'],
        p['w_r'].astype(BF16), p['w_k'].astype(BF16), p['w_v'].astype(BF16),
        jnp.concatenate([p['decay_w1'][0], p['decay_w1'][1]], axis=1).astype(BF16),
        _block_diag2(p['decay_w2'][0], p['decay_w2'][1]).astype(BF16),
        p['decay_w0'].reshape(1, 2 * d),
        jnp.concatenate([p['aaa_a1'][0], p['aaa_a1'][1]], axis=1).astype(BF16),
        _block_diag2(p['aaa_a2'][0], p['aaa_a2'][1]).astype(BF16),
        p['aaa_a0'].reshape(1, 2 * d),
        jnp.pad(p['gate_g1'], ((0, 0), (0, gate_pad))).astype(BF16),
        jnp.pad(p['gate_g2'], ((0, gate_pad), (0, 0))).astype(BF16),
        p['k_k'][None], p['k_a'][None], p['r_k'].reshape(1, d),
        _head_indicator(RWKV_HEAD, 1.0),
    ]
    tok = pl.BlockSpec((1, tm, d), lambda bi, ti: (bi, ti, 0))
    tok2 = pl.BlockSpec((2, 1, tm, d), lambda bi, ti: (0, bi, ti, 0))
    s1 = jax.ShapeDtypeStruct((b, t, d), F32)
    s2 = jax.ShapeDtypeStruct((2, b, t, d), F32)
    return pl.pallas_call(
        functools.partial(_rwkv_pre_kernel, n_ctx_tiles=n_ctx, n_tiles=n_tiles),
        grid=(b, n_tiles),
        in_specs=_halo_specs(tm, d, t // HALO) + [pl.BlockSpec((1, N_MOD, d), mod_map)]
        + [_const_spec(c.shape) for c in consts],
        out_specs=[tok, tok, tok, tok, tok2, tok2, tok, tok],
        out_shape=[s1, s1, s1, s1, s2, s2, s1, s1],
        compiler_params=_cparams(("parallel", "parallel")),
    )(xa, xa, xa, mods, *consts)


def _head_stack(x, lane_head):
    return jnp.concatenate([jnp.where(lane_head == hh, x, 0.0) for hh in range(HEADS_PER_GROUP)], axis=0)


def _wkv_kernel(rf_ref, vf_ref, kf_ref, kkf_ref, af_ref, ldf_ref, rb_ref, vb_ref, kb_ref, kkb_ref, ab_ref, ldb_ref,
                ka_ref, yf_ref, yb_ref, state_ref):
    gw = HEADS_PER_GROUP * RWKV_HEAD
    n_groups = rf_ref.shape[2] // gw

    @pl.when(pl.program_id(1) == 0)
    def _():
        state_ref[...] = jnp.zeros(state_ref.shape, F32)

    row = lax.broadcasted_iota(jnp.int32, (CHUNK, gw), 0)
    lane = lax.broadcasted_iota(jnp.int32, (CHUNK, gw), 1)
    col = lane % CHUNK
    lane_head = lane // RWKV_HEAD
    eye = (row == col).astype(F32)
    bd_r = lax.broadcasted_iota(jnp.int32, (gw, gw), 0) // RWKV_HEAD
    bd_c = lax.broadcasted_iota(jnp.int32, (gw, gw), 1) // RWKV_HEAD
    block_diag = bd_r == bd_c
    ka = ka_ref[...]
    stack = lambda x: _head_stack(x, lane_head)
    cols = [slice(gi * gw, (gi + 1) * gw) for gi in range(n_groups)]

    n_sub = rf_ref.shape[1] // CHUNK
    wide = {}
    for di, (refs, incl, last) in enumerate((((rf_ref, vf_ref, kf_ref, kkf_ref, af_ref, ldf_ref), row >= col, CHUNK - 1),
                                             ((rb_ref, vb_ref, kb_ref, kkb_ref, ab_ref, ldb_ref), row <= col, 0))):
        r_ref, v_ref, k_ref, kk_ref, a_ref, ld_ref = refs
        tri = jnp.where(incl[:, :CHUNK], 1.0, 0.0).astype(BF16)
        for wi in range(n_sub):
            ci = wi if di == 0 else n_sub - 1 - wi
            rows = slice(ci * CHUNK, (ci + 1) * CHUNK)
            ld = ld_ref[0, 0, rows]
            ag = a_ref[0, 0, rows]
            kk = kk_ref[0, rows]
            kc = k_ref[0, rows] * (1.0 + (ag - 1.0) * ka)
            bc = kk * ag
            log_p = _dot_split_lhs_rhs(tri, ld)
            log_p_last = log_p[last:last + 1]
            dec_out = jnp.exp(-log_p)
            dec_tail = jnp.exp(log_p_last - log_p)
            wide[di, wi] = dict(
                rows=rows, incl=incl, strict=incl & (row != col), vc=v_ref[0, rows],
                ar=jnp.concatenate([-kk * jnp.exp(log_p - ld), r_ref[0, rows] * jnp.exp(log_p)], axis=0),
                b_out=bc * dec_out, k_out=kc * dec_out,
                bk=jnp.concatenate([bc * dec_tail, kc * dec_tail], axis=0),
                state_decay=jnp.exp(log_p_last))

    chains = [(di, wi, gi) for wi in range(n_sub) for di in range(2) for gi in range(n_groups)]
    w = lambda ch, name: wide[ch[0], ch[1]][name]
    part = lambda ch, name: wide[ch[0], ch[1]][name][:, cols[ch[2]]]
    xb = [_dot_nt(part(ch, 'ar'), stack(part(ch, 'b_out'))) for ch in chains]
    xk = [_dot_nt(part(ch, 'ar'), stack(part(ch, 'k_out'))) for ch in chains]
    a_ab = [jnp.where(w(ch, 'strict'), x[:CHUNK], 0.0) for ch, x in zip(chains, xb)]
    a_rb = [jnp.where(w(ch, 'incl'), x[CHUNK:], 0.0) for ch, x in zip(chains, xb)]
    a_kk = [jnp.concatenate([jnp.where(w(ch, 'strict'), x[:CHUNK], 0.0), jnp.where(w(ch, 'incl'), x[CHUNK:], 0.0)],
                            axis=0) for ch, x in zip(chains, xk)]
    av = [_dot(a, stack(part(ch, 'vc'))) for ch, a in zip(chains, a_kk)]

    m = [eye + a for a in a_ab]
    a_pow = [_dot(a, stack(a)) for a in a_ab]
    n_sq = int(math.log2(CHUNK)) - 1
    for i in range(n_sq - 1):
        both = [_dot(jnp.concatenate([p, mm], axis=0), stack(p)) for p, mm in zip(a_pow, m)]
        a_pow = [x[:CHUNK] for x in both]
        m = [mm + x[CHUNK:] for mm, x in zip(m, both)]
    m = [mm + _dot(mm, stack(p)) for mm, p in zip(m, a_pow)]

    state = {(di, gi): state_ref[di * n_groups + gi] for di in range(2) for gi in range(n_groups)}
    for wi in range(n_sub):
        idx = [i for i, ch in enumerate(chains) if ch[1] == wi]
        arh = {i: _dot_nt(part(chains[i], 'ar'), state[chains[i][0], chains[i][2]]) for i in idx}
        u = {i: _dot(m[i], stack(arh[i][:CHUNK] + av[i][:CHUNK])) for i in idx}
        y = {i: arh[i][CHUNK:] + _dot(a_rb[i], stack(u[i])) + av[i][CHUNK:] for i in idx}
        upd = {i: _dot_tn(jnp.concatenate([u[i], part(chains[i], 'vc')], axis=0), part(chains[i], 'bk')) for i in idx}
        for di, y_ref in ((0, yf_ref), (1, yb_ref)):
            y_ref[0, wide[di, wi]['rows']] = jnp.concatenate([y[i] for i in idx if chains[i][0] == di], axis=1)
        for i in idx:
            di, _, gi = chains[i]
            state[di, gi] = part(chains[i], 'state_decay') * state[di, gi] + jnp.where(block_diag, upd[i], 0.0)
    for (di, gi), s in state.items():
        state_ref[di * n_groups + gi] = s


def _dot_split_lhs_rhs(tri_bf16, x):
    hi, lo = _split(x)
    return (jnp.dot(tri_bf16, hi, preferred_element_type=F32)
            + jnp.dot(tri_bf16, lo, preferred_element_type=F32))


def _wkv(r, v, k, kk, a, ld, k_a, ctx_len):
    b, t, d = r.shape
    rows = WKV_STEP_CHUNKS * CHUNK
    assert ctx_len % rows == 0 and t % rows == 0
    n_steps = t // rows
    n_ctx = ctx_len // rows

    def back(ci):
        return jnp.where(ci < n_ctx, n_ctx - 1 - ci, n_steps - 1 - (ci - n_ctx))

    fwd = pl.BlockSpec((1, rows, d), lambda bi, ci: (bi, ci, 0))
    bwd = pl.BlockSpec((1, rows, d), lambda bi, ci: (bi, back(ci), 0))
    fwd2 = pl.BlockSpec((1, 1, rows, d), lambda bi, ci: (0, bi, ci, 0))
    bwd2 = pl.BlockSpec((1, 1, rows, d), lambda bi, ci: (1, bi, back(ci), 0))
    gw = HEADS_PER_GROUP * RWKV_HEAD
    y_shape = jax.ShapeDtypeStruct((b, t, d), F32)
    return pl.pallas_call(
        _wkv_kernel,
        grid=(b, n_steps),
        in_specs=[fwd, fwd, fwd, fwd, fwd2, fwd2, bwd, bwd, bwd, bwd, bwd2, bwd2, _const_spec((1, d))],
        out_specs=[fwd, bwd],
        out_shape=[y_shape, y_shape],
        scratch_shapes=[pltpu.VMEM((2 * d // gw, gw, gw), F32)],
        compiler_params=_cparams(("arbitrary", "arbitrary")),
    )(r, v, k, kk, a, ld, r, v, k, kk, a, ld, k_a[None])


def _rwkv_out_kernel(x_ref, yf_ref, yb_ref, bv_ref, g_ref, mod_ref, gnw_ref, gnb_ref, gmean_ref, wo_ref, out_ref):
    y = yf_ref[0] + yb_ref[0]
    gmean = gmean_ref[...]
    dev = y - _head_sum(y, gmean)
    var = _head_sum(dev * dev, gmean)
    y_n = dev * lax.rsqrt(var + GN_EPS) * gnw_ref[...] + gnb_ref[...]
    z = (y_n + bv_ref[0]) * g_ref[0]
    out_ref[0] = x_ref[0] + mod_ref[0][2:3] * _dot(z, wo_ref[...])


def _rwkv_out(xa, y, bv, g, mods, gn_w, gn_b, w_o, n_ctx_tiles):
    b, t, d = xa.shape
    tm = TOKEN_TILE
    s = t - n_ctx_tiles * tm
    gmean = _head_indicator(RWKV_HEAD, 1.0 / RWKV_HEAD)
    lat = pl.BlockSpec((1, tm, d), lambda bi, ti: (bi, ti + n_ctx_tiles, 0))
    return pl.pallas_call(
        _rwkv_out_kernel,
        grid=(b, s // tm),
        in_specs=[
            lat, lat, lat, lat, lat,
            pl.BlockSpec((1, N_MOD, d), lambda bi, ti: (bi, 0, 0)),
            _const_spec((1, d)), _const_spec((1, d)), _const_spec((MXU_WIDTH, MXU_WIDTH)), _const_spec((d, d)),
        ],
        out_specs=pl.BlockSpec((1, tm, d), lambda bi, ti: (bi, ti, 0)),
        out_shape=jax.ShapeDtypeStruct((b, s, d), F32),
        compiler_params=_cparams(("parallel", "parallel")),
    )(xa, y[0], y[1], bv, g, mods, gn_w[None], gn_b[None], gmean, w_o.astype(BF16))


def _rope_tables(seq, ctx_len):
    rows = seq // GRID_W
    row = jnp.repeat(jnp.arange(rows, dtype=F32), GRID_W)
    col = jnp.tile(jnp.arange(GRID_W, dtype=F32), rows)
    n_freq = HEAD_DIM // 4
    inv_freq = ROPE_THETA ** (-jnp.arange(n_freq, dtype=F32) / n_freq)
    ang = jnp.concatenate([row[:, None] * inv_freq, col[:, None] * inv_freq], axis=-1)
    cos = jnp.repeat(jnp.cos(ang), 2, axis=1)
    sin = jnp.repeat(jnp.sin(ang), 2, axis=1) * jnp.tile(jnp.array([-1.0, 1.0], F32), HEAD_DIM // 2)
    cos = jnp.concatenate([jnp.ones((ctx_len, HEAD_DIM), F32), cos], axis=0)
    sin = jnp.concatenate([jnp.zeros((ctx_len, HEAD_DIM), F32), sin], axis=0)
    reps = LANES // HEAD_DIM
    return jnp.tile(cos, (1, reps)), jnp.tile(sin, (1, reps))


def kernel(x, c, ctx, c_ctx, ada_w, ada_b, attn_w_qkv, attn_q_gain, attn_k_gain, attn_w_o, rwkv_mu, rwkv_w_r, rwkv_w_k, rwkv_w_v, rwkv_w_o, rwkv_decay_w0, rwkv_decay_w1, rwkv_decay_w2, rwkv_aaa_a0, rwkv_aaa_a1, rwkv_aaa_a2, rwkv_gate_g1, rwkv_gate_g2, rwkv_k_k, rwkv_k_a, rwkv_r_k, rwkv_gn_w, rwkv_gn_b, ffn_w_up, ffn_conv_w, ffn_conv_b, ffn_w_down, final_gain):
    b, seq, d = x.shape
    ctx_len = ctx.shape[1]
    assert ada_w.shape[0] == 2 and attn_w_qkv.shape[0] == 1 and rwkv_w_r.shape[0] == 1
    assert b + 1 <= MOD_ROWS and ctx_len % TOKEN_TILE == 0 and seq % TOKEN_TILE == 0 and seq % GRID_W == 0
    n_ctx_tiles = ctx_len // TOKEN_TILE

    cc = jnp.zeros((MOD_ROWS, d), F32).at[:b].set(c).at[b].set(c_ctx)
    mods = _ada_mods(cc, ada_w, ada_b)
    xa = jnp.concatenate([ctx, x], axis=1)

    cos_t, sin_t = _rope_tables(seq, ctx_len)
    q, kt, v = _attn_pre(xa, mods[0], attn_w_qkv[0], attn_q_gain[0], attn_k_gain[0], cos_t, sin_t, n_ctx_tiles)
    o = _flash(q, kt, v, ctx_len)
    xa = _attn_out(xa, o, mods[0], attn_w_o[0], n_ctx_tiles)
    xa = _conv_ffn(xa, mods[0], ffn_w_up[0], ffn_conv_w[0], ffn_conv_b[0], ffn_w_down[0], n_ctx_tiles, TOKEN_TILE)

    p = dict(mu=rwkv_mu[0], w_r=rwkv_w_r[0], w_k=rwkv_w_k[0], w_v=rwkv_w_v[0],
             decay_w0=rwkv_decay_w0[0], decay_w1=rwkv_decay_w1[0], decay_w2=rwkv_decay_w2[0],
             aaa_a0=rwkv_aaa_a0[0], aaa_a1=rwkv_aaa_a1[0], aaa_a2=rwkv_aaa_a2[0],
             gate_g1=rwkv_gate_g1[0], gate_g2=rwkv_gate_g2[0],
             k_k=rwkv_k_k[0], k_a=rwkv_k_a[0], r_k=rwkv_r_k[0])
    r, vv, k, kk, a, ld, g, bv = _rwkv_pre(xa, mods[1], p, n_ctx_tiles)
    y = _wkv(r, vv, k, kk, a, ld, rwkv_k_a[0], ctx_len)
    xl = _rwkv_out(xa, y, bv, g, mods[1], rwkv_gn_w[0], rwkv_gn_b[0], rwkv_w_o[0], n_ctx_tiles)
    return _conv_ffn(xl, mods[1], ffn_w_up[1], ffn_conv_w[1], ffn_conv_b[1], ffn_w_down[1], 0, TOKEN_TILE,
                     final_gain=final_gain)
```

```python
import functools
import math

import jax
import jax.numpy as jnp
from jax import lax
from jax.experimental import pallas as pl
from jax.experimental.pallas import tpu as pltpu

F32 = jnp.float32
BF16 = jnp.bfloat16

LANES = 128
SUBLANES = 8
MXU_WIDTH = 256
VMEM_LIMIT_BYTES = 56 * 1024 * 1024

N_HEADS = 16
N_KV_HEADS = 4
HEAD_DIM = 64
GROUP = N_HEADS // N_KV_HEADS
RWKV_HEAD = 64
GRID_W = 64
ROPE_THETA = 10000.0
NORM_EPS = 1e-6
GN_EPS = 64e-5
N_MOD = 6
MOD_ROWS = 8

TOKEN_TILE = 256
RWKV_TILE = 256
Q_TILE = 256
KV_TILE = 2816
KV_HEADS_PER_STEP = 1
FF_CHUNK = 256
CHUNK = 64
HEADS_PER_GROUP = 4
WKV_STEP_CHUNKS = 2
HALO = SUBLANES


def _cparams(sem):
    return pltpu.CompilerParams(dimension_semantics=sem, vmem_limit_bytes=VMEM_LIMIT_BYTES)


def _dot(a, b):
    return jnp.dot(a.astype(BF16), b.astype(BF16), preferred_element_type=F32)


def _dot_nt(a, b):
    return lax.dot_general(a.astype(BF16), b.astype(BF16), (((1,), (1,)), ((), ())),
                           preferred_element_type=F32)


def _dot_tn(a, b):
    return lax.dot_general(a.astype(BF16), b.astype(BF16), (((0,), (0,)), ((), ())),
                           preferred_element_type=F32)


def _split(a):
    hi = a.astype(BF16)
    lo = (a - hi.astype(F32)).astype(BF16)
    return hi, lo


def _head_indicator(head_dim, scale):
    return jnp.kron(jnp.eye(MXU_WIDTH // head_dim, dtype=F32), jnp.full((head_dim, head_dim), scale, F32)).astype(BF16)


def _head_sum(a, indicator):
    a = a.astype(BF16)
    blocks = [jnp.dot(a[:, j:j + MXU_WIDTH], indicator, preferred_element_type=F32)
              for j in range(0, a.shape[1], MXU_WIDTH)]
    return blocks[0] if len(blocks) == 1 else jnp.concatenate(blocks, axis=1)


def _dot_split_both(a, b):
    ah, al = _split(a)
    bh, bl = _split(b)
    d = functools.partial(jnp.dot, preferred_element_type=F32)
    return d(ah, bh) + (d(ah, bl) + d(al, bh))


def _sigmoid(x):
    return 1.0 / (1.0 + jnp.exp(-x))


def _rms_norm(x):
    return x * lax.rsqrt(jnp.mean(x * x, axis=-1, keepdims=True) + NORM_EPS)


def _modulate(x, shift, scale):
    return _rms_norm(x) * (1.0 + scale) + shift


def _const_spec(shape):
    nd = len(shape)
    return pl.BlockSpec(shape, lambda *_: (0,) * nd)


def _ada_kernel(c_ref, w_ref, b_ref, o_ref):
    cc = c_ref[...]
    o_ref[0, 0] = _dot_split_both(cc * _sigmoid(cc), w_ref[0]) + b_ref[0, 0]


def _ada_mods(cc, ada_w, ada_b):
    depth, d, _ = ada_w.shape
    out = pl.pallas_call(
        _ada_kernel,
        grid=(depth, N_MOD),
        in_specs=[
            pl.BlockSpec((MOD_ROWS, d), lambda i, n: (0, 0)),
            pl.BlockSpec((1, d, d), lambda i, n: (i, 0, n)),
            pl.BlockSpec((1, 1, 1, d), lambda i, n: (i, n, 0, 0)),
        ],
        out_specs=pl.BlockSpec((1, 1, MOD_ROWS, d), lambda i, n: (i, n, 0, 0)),
        out_shape=jax.ShapeDtypeStruct((depth, N_MOD, MOD_ROWS, d), F32),
        compiler_params=_cparams(("arbitrary", "arbitrary")),
    )(cc, ada_w, ada_b.reshape(depth, N_MOD, 1, d))
    return out.transpose(0, 2, 1, 3)


def _rope(x, cos, sin):
    width = x.shape[1]
    reps = width // LANES
    cc = jnp.concatenate([cos] * reps, axis=1)
    ss = jnp.concatenate([sin] * reps, axis=1)
    lane = lax.broadcasted_iota(jnp.int32, x.shape, 1)
    partner = jnp.where((lane & 1) == 0, pltpu.roll(x, width - 1, 1), pltpu.roll(x, 1, 1))
    return x * cc + partner * ss


def _token_tile_specs(tm, d, n_ctx_tiles):
    return [pl.BlockSpec((1, tm, d), lambda bi, ti: (bi, jnp.minimum(ti, n_ctx_tiles - 1), 0)),
            pl.BlockSpec((1, tm, d), lambda bi, ti: (bi, jnp.maximum(ti - n_ctx_tiles, 0), 0))]


def _token_tile(c_ref, x_ref, n_ctx_tiles):
    return jnp.where(pl.program_id(1) < n_ctx_tiles, c_ref[0], x_ref[0])


def _attn_pre_kernel(c_ref, x_ref, mod_ref, w_ref, qg_ref, kg_ref, hmean_ref, cos_ref, sin_ref,
                     q_ref, kt_ref, v_ref, *, q_dim, kv_dim, n_ctx_tiles):
    mod = mod_ref[0]
    h = _modulate(_token_tile(c_ref, x_ref, n_ctx_tiles), mod[0:1], mod[1:2])
    qkv = _dot(h, w_ref[...])
    q = qkv[:, :q_dim]
    k = qkv[:, q_dim:q_dim + kv_dim]
    v = qkv[:, q_dim + kv_dim:]
    hmean = hmean_ref[...]
    q = q * lax.rsqrt(_head_sum(q * q, hmean) + NORM_EPS) * qg_ref[...]
    k = k * lax.rsqrt(_head_sum(k * k, hmean) + NORM_EPS) * kg_ref[...]
    cos = cos_ref[...]
    sin = sin_ref[...]
    q = _rope(q, cos, sin) * (HEAD_DIM ** -0.5 * math.log2(math.e))
    k = _rope(k, cos, sin)
    q_ref[0] = q.astype(BF16)
    kt_ref[0] = k.T.astype(BF16)
    rows = v.shape[0]
    one_col = (lax.broadcasted_iota(jnp.int32, (rows, HEAD_DIM), 1) == 0).astype(F32)
    for g in range(N_KV_HEADS):
        v_ref[0, g] = jnp.concatenate([v[:, g * HEAD_DIM:(g + 1) * HEAD_DIM], one_col], axis=1).astype(BF16)


def _attn_pre(ctx, x, mods, w_qkv, q_gain, k_gain, cos_t, sin_t, n_ctx_tiles):
    b, seq, d = x.shape
    t = ctx.shape[1] + seq
    q_dim = N_HEADS * HEAD_DIM
    kv_dim = N_KV_HEADS * HEAD_DIM
    tm = TOKEN_TILE
    nb = b
    mod_map = lambda bi, ti: (jnp.where(ti < n_ctx_tiles, nb, bi), 0, 0)
    return pl.pallas_call(
        functools.partial(_attn_pre_kernel, q_dim=q_dim, kv_dim=kv_dim, n_ctx_tiles=n_ctx_tiles),
        grid=(b, t // tm),
        in_specs=_token_tile_specs(tm, d, n_ctx_tiles) + [
            pl.BlockSpec((1, N_MOD, d), mod_map),
            _const_spec((d, q_dim + 2 * kv_dim)),
            _const_spec((1, q_dim)),
            _const_spec((1, kv_dim)),
            _const_spec((MXU_WIDTH, MXU_WIDTH)),
            pl.BlockSpec((tm, LANES), lambda bi, ti: (ti, 0)),
            pl.BlockSpec((tm, LANES), lambda bi, ti: (ti, 0)),
        ],
        out_specs=[
            pl.BlockSpec((1, tm, q_dim), lambda bi, ti: (bi, ti, 0)),
            pl.BlockSpec((1, kv_dim, tm), lambda bi, ti: (bi, 0, ti)),
            pl.BlockSpec((1, N_KV_HEADS, tm, LANES), lambda bi, ti: (bi, 0, ti, 0)),
        ],
        out_shape=[
            jax.ShapeDtypeStruct((b, t, q_dim), BF16),
            jax.ShapeDtypeStruct((b, kv_dim, t), BF16),
            jax.ShapeDtypeStruct((b, N_KV_HEADS, t, LANES), BF16),
        ],
        compiler_params=_cparams(("parallel", "parallel")),
    )(ctx, x, mods, w_qkv.astype(BF16), jnp.tile(q_gain, N_HEADS)[None], jnp.tile(k_gain, N_KV_HEADS)[None],
      _head_indicator(HEAD_DIM, 1.0 / HEAD_DIM), cos_t, sin_t)


def _flash_kernel(q_ref, kt_ref, v_ref, o_ref, q4_ref, *, n_ctx_q, ctx_len, kv_tiles):
    qi = pl.program_id(2)
    tq = q_ref.shape[1]
    n_heads = q_ref.shape[2] // HEAD_DIM
    q = q_ref[0]
    for h in range(n_heads):
        q4_ref[h * tq:(h + 1) * tq] = q[:, h * HEAD_DIM:(h + 1) * HEAD_DIM]

    def attend(tiles):
        steps = [(start, size, g * GROUP + j) for g in range(n_heads // GROUP) for start, size in tiles
                 for j in range(GROUP)]

        def scores(st):
            start, size, h = st
            g = h // GROUP
            return jnp.dot(q4_ref[h * tq:(h + 1) * tq], kt_ref[0, g * HEAD_DIM:(g + 1) * HEAD_DIM, start:start + size],
                           preferred_element_type=F32)

        m = [None] * n_heads
        acc = [None] * n_heads
        s_next = scores(steps[0])
        for i, (start, size, h) in enumerate(steps):
            s = s_next
            if i + 1 < len(steps):
                s_next = scores(steps[i + 1])
            cols = [s[:, t * LANES:(t + 1) * LANES] for t in range(size // LANES)]
            m_cur = jnp.max(functools.reduce(jnp.maximum, cols), axis=1, keepdims=True)
            m_new = jnp.broadcast_to(m_cur, (tq, LANES)) if m[h] is None else jnp.maximum(m[h], m_cur)
            p = jnp.concatenate([jnp.exp2(c - m_new) for c in cols], axis=1).astype(BF16)
            pv = jnp.dot(p, v_ref[0, h // GROUP, start:start + size, :], preferred_element_type=F32)
            acc[h] = pv if acc[h] is None else jnp.exp2(m[h] - m_new) * acc[h] + pv
            m[h] = m_new
        o = [a[:, :HEAD_DIM] / a[:, HEAD_DIM:HEAD_DIM + 1] for a in acc]
        o_ref[0] = jnp.concatenate(o, axis=1).astype(BF16)

    @pl.when(qi < n_ctx_q)
    def _():
        attend([(0, ctx_len)])

    @pl.when(qi >= n_ctx_q)
    def _():
        attend(kv_tiles)


def _flash(q, kt, v, ctx_len):
    b, t, q_dim = q.shape
    tq = Q_TILE
    kvh = KV_HEADS_PER_STEP
    gw = kvh * GROUP * HEAD_DIM
    n_kv = next(n for n in range(1, t // LANES + 1) if t % (n * LANES) == 0 and t // n <= KV_TILE)
    kv_tiles = [(i * (t // n_kv), t // n_kv) for i in range(n_kv)]
    kern = functools.partial(_flash_kernel, n_ctx_q=ctx_len // tq, ctx_len=ctx_len, kv_tiles=kv_tiles)
    return pl.pallas_call(
        kern,
        grid=(b, N_KV_HEADS // kvh, t // tq),
        in_specs=[
            pl.BlockSpec((1, tq, gw), lambda bi, g, qi: (bi, qi, g)),
            pl.BlockSpec((1, kvh * HEAD_DIM, t), lambda bi, g, qi: (bi, g, 0)),
            pl.BlockSpec((1, kvh, t, LANES), lambda bi, g, qi: (bi, g, 0, 0)),
        ],
        out_specs=pl.BlockSpec((1, tq, gw), lambda bi, g, qi: (bi, qi, g)),
        out_shape=jax.ShapeDtypeStruct((b, t, q_dim), BF16),
        scratch_shapes=[pltpu.VMEM((kvh * GROUP * tq, HEAD_DIM), BF16)],
        compiler_params=_cparams(("parallel", "parallel", "arbitrary")),
    )(q, kt, v)


def _attn_out_kernel(c_ref, x_ref, o_ref, mod_ref, w_ref, out_ref, *, n_ctx_tiles):
    y = jnp.dot(o_ref[0], w_ref[...], preferred_element_type=F32)
    out_ref[0] = _token_tile(c_ref, x_ref, n_ctx_tiles) + mod_ref[0][2:3] * y


def _attn_out(ctx, x, o, mods, w_o, n_ctx_tiles):
    b, seq, d = x.shape
    t = ctx.shape[1] + seq
    tm = TOKEN_TILE
    nb = b
    mod_map = lambda bi, ti: (jnp.where(ti < n_ctx_tiles, nb, bi), 0, 0)
    return pl.pallas_call(
        functools.partial(_attn_out_kernel, n_ctx_tiles=n_ctx_tiles),
        grid=(b, t // tm),
        in_specs=_token_tile_specs(tm, d, n_ctx_tiles) + [
            pl.BlockSpec((1, tm, o.shape[2]), lambda bi, ti: (bi, ti, 0)),
            pl.BlockSpec((1, N_MOD, d), mod_map),
            _const_spec(w_o.shape),
        ],
        out_specs=pl.BlockSpec((1, tm, d), lambda bi, ti: (bi, ti, 0)),
        out_shape=jax.ShapeDtypeStruct((b, t, d), F32),
        compiler_params=_cparams(("parallel", "parallel")),
    )(ctx, x, o, mods, w_o.astype(BF16))


def _halo_specs(tm, d, n_row_blocks):
    per = tm // HALO
    return [
        pl.BlockSpec((1, tm, d), lambda bi, ti: (bi, ti, 0)),
        pl.BlockSpec((1, HALO, d), lambda bi, ti: (bi, jnp.maximum(ti * per - 1, 0), 0)),
        pl.BlockSpec((1, HALO, d), lambda bi, ti: (bi, jnp.minimum((ti + 1) * per, n_row_blocks - 1), 0)),
    ]


def _edge_keep(ti, n_ctx_tiles, n_tiles):
    first = (ti == 0) | (ti == n_ctx_tiles)
    last = (ti == n_tiles - 1) | (ti == n_ctx_tiles - 1)
    return jnp.where(first, 0.0, 1.0).astype(F32), jnp.where(last, 0.0, 1.0).astype(F32)


def _ffn_kernel(*refs, n_ctx_tiles, n_tiles, n_fc, final):
    if final:
        xm_ref, xp_ref, xn_ref, mod_ref, wup_ref, cw_ref, cb_ref, wdn_ref, fg_ref, out_ref = refs
    else:
        xm_ref, xp_ref, xn_ref, mod_ref, wup_ref, cw_ref, cb_ref, wdn_ref, out_ref = refs
    tm = xm_ref.shape[1]
    ext = tm + 2 * HALO
    mod = mod_ref[0]
    shift, scale, gate = mod[3:4], mod[4:5], mod[5:6]
    keep_prev, keep_next = _edge_keep(pl.program_id(1), n_ctx_tiles, n_tiles)
    xm = xm_ref[0]
    h = jnp.concatenate([_modulate(xp_ref[0], shift, scale) * keep_prev,
                         _modulate(xm, shift, scale),
                         _modulate(xn_ref[0], shift, scale) * keep_next], axis=0).astype(BF16)

    def conv(u, cw, cb):
        y = cw[0:1] * pltpu.roll(u, 1, 0) + cw[1:2] * u + cw[2:3] * pltpu.roll(u, ext - 1, 0) + cb
        return y[HALO:HALO + tm]

    col = lambda j: slice(j * FF_CHUNK, (j + 1) * FF_CHUNK)
    up = lambda j: (jnp.dot(h, wup_ref[:, col(j)], preferred_element_type=F32),
                    jnp.dot(h, wup_ref[:, col(n_fc + j)], preferred_element_type=F32))
    acc = None
    nxt = up(0)
    for j in range(n_fc):
        ug, uv = nxt
        if j + 1 < n_fc:
            nxt = up(j + 1)
        cg = conv(ug, cw_ref[:, col(j)], cb_ref[:, col(j)])
        cv = conv(uv, cw_ref[:, col(n_fc + j)], cb_ref[:, col(n_fc + j)])
        act = cg * _sigmoid(cg) * cv
        down = jnp.dot(act.astype(BF16), wdn_ref[col(j), :], preferred_element_type=F32)
        acc = down if acc is None else acc + down

    out = xm + gate * acc
    if final:
        out = _rms_norm(out) * fg_ref[...]
    out_ref[0] = out


def _conv_ffn(xa, mods, w_up, conv_w, conv_b, w_down, n_ctx_tiles, tm, final_gain=None):
    b, t, d = xa.shape
    d_ff = w_down.shape[0]
    n_fc = d_ff // FF_CHUNK
    n_tiles = t // tm
    nb = b
    final = final_gain is not None
    wup = w_up.astype(BF16)
    cw = conv_w
    cb = conv_b[None]
    wdn = w_down.astype(BF16)
    mod_map = lambda bi, ti: (jnp.where(ti < n_ctx_tiles, nb, bi), 0, 0)
    in_specs = _halo_specs(tm, d, t // HALO) + [
        pl.BlockSpec((1, N_MOD, d), mod_map),
        _const_spec(wup.shape), _const_spec(cw.shape), _const_spec(cb.shape), _const_spec(wdn.shape),
    ]
    args = [xa, xa, xa, mods, wup, cw, cb, wdn]
    if final:
        in_specs.append(_const_spec((1, d)))
        args.append(final_gain[None])
    return pl.pallas_call(
        functools.partial(_ffn_kernel, n_ctx_tiles=n_ctx_tiles, n_tiles=n_tiles, n_fc=n_fc, final=final),
        grid=(b, n_tiles),
        in_specs=in_specs,
        out_specs=pl.BlockSpec((1, tm, d), lambda bi, ti: (bi, ti, 0)),
        out_shape=jax.ShapeDtypeStruct((b, t, d), F32),
        compiler_params=_cparams(("parallel", "parallel")),
    )(*args)


def _rwkv_pre_kernel(xm_ref, xp_ref, xn_ref, mod_ref, mu_ref, wr_ref, wk_ref, wv_ref,
                     dw1_ref, dw2_ref, w0_ref, a1_ref, a2_ref, a0_ref, g1_ref, g2_ref,
                     kk_ref, ka_ref, rk_ref, gsum_ref,
                     r_out, v_out, k_out, kk_out, a_out, ld_out, g_out, bv_out,
                     *, n_ctx_tiles, n_tiles):
    tm = xm_ref.shape[1]
    d = xm_ref.shape[2]
    ext = tm + 2 * HALO
    mod = mod_ref[0]
    shift, scale = mod[0:1], mod[1:2]
    keep_prev, keep_next = _edge_keep(pl.program_id(1), n_ctx_tiles, n_tiles)
    h = _modulate(xm_ref[0], shift, scale)
    he = jnp.concatenate([_modulate(xp_ref[0], shift, scale) * keep_prev, h,
                          _modulate(xn_ref[0], shift, scale) * keep_next], axis=0)
    xx = 0.5 * (pltpu.roll(he, 1, 0) + pltpu.roll(he, ext - 1, 0))[HALO:HALO + tm] - h
    mu = mu_ref[...]
    mix = lambda i: h + xx * mu[i:i + 1]
    r = _dot(mix(0), wr_ref[...])
    k = _dot(mix(2), wk_ref[...])
    v = _dot(mix(3), wv_ref[...])
    z = w0_ref[...] + _dot(jnp.tanh(_dot(mix(1), dw1_ref[...])), dw2_ref[...])
    log_decay = -math.exp(-0.5) * _sigmoid(z)
    a = _sigmoid(a0_ref[...] + _dot(_dot(mix(4), a1_ref[...]), a2_ref[...]))
    g = _dot(_sigmoid(_dot(mix(5), g1_ref[...])), g2_ref[...])
    gsum = gsum_ref[...]
    kx = k * kk_ref[...]
    kk = kx / jnp.maximum(jnp.sqrt(_head_sum(kx * kx, gsum)), 1e-12)
    ka = ka_ref[...]
    bonus = (r * rk_ref[...]) * k * (2.0 + (a[:, :d] + a[:, d:] - 2.0) * ka)
    r_out[0] = r
    v_out[0] = v.astype(BF16)
    k_out[0] = k
    kk_out[0] = kk
    a_out[0, 0] = a[:, :d]
    a_out[1, 0] = a[:, d:]
    ld_out[0, 0] = log_decay[:, :d]
    ld_out[1, 0] = log_decay[:, d:]
    g_out[0] = g.astype(BF16)
    bv_out[0] = (_head_sum(bonus, gsum) * v).astype(BF16)


def _block_diag2(m0, m1):
    z = jnp.zeros_like(m0)
    return jnp.concatenate([jnp.concatenate([m0, z], axis=1), jnp.concatenate([z, m1], axis=1)], axis=0)


def _rwkv_pre(xa, mods, p, n_ctx_tiles):
    b, t, d = xa.shape
    tm = RWKV_TILE
    n_tiles = t // tm
    nb = b
    n_ctx = n_ctx_tiles * (TOKEN_TILE // tm)
    mod_map = lambda bi, ti: (jnp.where(ti < n_ctx, nb, bi), 0, 0)
    gate_pad = -p['gate_g1'].shape[1] % LANES
    consts = [
        p['mu'],
        p['w_r'].astype(BF16), p['w_k'].astype(BF16), p['w_v'].astype(BF16),
        jnp.concatenate([p['decay_w1'][0], p['decay_w1'][1]], axis=1).astype(BF16),
        _block_diag2(p['decay_w2'][0], p['decay_w2'][1]).astype(BF16),
        p['decay_w0'].reshape(1, 2 * d),
        jnp.concatenate([p['aaa_a1'][0], p['aaa_a1'][1]], axis=1).astype(BF16),
        _block_diag2(p['aaa_a2'][0], p['aaa_a2'][1]).astype(BF16),
        p['aaa_a0'].reshape(1, 2 * d),
        jnp.pad(p['gate_g1'], ((0, 0), (0, gate_pad))).astype(BF16),
        jnp.pad(p['gate_g2'], ((0, gate_pad), (0, 0))).astype(BF16),
        p['k_k'][None], p['k_a'][None], p['r_k'].reshape(1, d),
        _head_indicator(RWKV_HEAD, 1.0),
    ]
    tok = pl.BlockSpec((1, tm, d), lambda bi, ti: (bi, ti, 0))
    tok2 = pl.BlockSpec((2, 1, tm, d), lambda bi, ti: (0, bi, ti, 0))
    s1 = jax.ShapeDtypeStruct((b, t, d), F32)
    s2 = jax.ShapeDtypeStruct((2, b, t, d), F32)
    h1 = jax.ShapeDtypeStruct((b, t, d), BF16)
    return pl.pallas_call(
        functools.partial(_rwkv_pre_kernel, n_ctx_tiles=n_ctx, n_tiles=n_tiles),
        grid=(b, n_tiles),
        in_specs=_halo_specs(tm, d, t // HALO) + [pl.BlockSpec((1, N_MOD, d), mod_map)]
        + [_const_spec(c.shape) for c in consts],
        out_specs=[tok, tok, tok, tok, tok2, tok2, tok, tok],
        out_shape=[s1, h1, s1, s1, s2, s2, h1, h1],
        compiler_params=_cparams(("parallel", "parallel")),
    )(xa, xa, xa, mods, *consts)


def _head_stack(x, lane_head):
    return jnp.concatenate([jnp.where(lane_head == hh, x, 0.0) for hh in range(HEADS_PER_GROUP)], axis=0)


def _wkv_kernel(rf_ref, vf_ref, kf_ref, kkf_ref, af_ref, ldf_ref, rb_ref, vb_ref, kb_ref, kkb_ref, ab_ref, ldb_ref,
                ka_ref, yf_ref, yb_ref, state_ref):
    gw = HEADS_PER_GROUP * RWKV_HEAD
    n_groups = rf_ref.shape[2] // gw

    @pl.when(pl.program_id(1) == 0)
    def _():
        state_ref[...] = jnp.zeros(state_ref.shape, F32)

    row = lax.broadcasted_iota(jnp.int32, (CHUNK, gw), 0)
    lane = lax.broadcasted_iota(jnp.int32, (CHUNK, gw), 1)
    col = lane % CHUNK
    lane_head = lane // RWKV_HEAD
    eye = (row == col).astype(F32)
    bd_r = lax.broadcasted_iota(jnp.int32, (gw, gw), 0) // RWKV_HEAD
    bd_c = lax.broadcasted_iota(jnp.int32, (gw, gw), 1) // RWKV_HEAD
    block_diag = bd_r == bd_c
    ka = ka_ref[...]
    stack = lambda x: _head_stack(x, lane_head)
    cols = [slice(gi * gw, (gi + 1) * gw) for gi in range(n_groups)]

    n_sub = rf_ref.shape[1] // CHUNK
    wide = {}
    for di, (refs, incl, last) in enumerate((((rf_ref, vf_ref, kf_ref, kkf_ref, af_ref, ldf_ref), row >= col, CHUNK - 1),
                                             ((rb_ref, vb_ref, kb_ref, kkb_ref, ab_ref, ldb_ref), row <= col, 0))):
        r_ref, v_ref, k_ref, kk_ref, a_ref, ld_ref = refs
        tri = jnp.where(incl[:, :CHUNK], 1.0, 0.0).astype(BF16)
        for wi in range(n_sub):
            ci = wi if di == 0 else n_sub - 1 - wi
            rows = slice(ci * CHUNK, (ci + 1) * CHUNK)
            ld = ld_ref[0, 0, rows]
            ag = a_ref[0, 0, rows]
            kk = kk_ref[0, rows]
            kc = k_ref[0, rows] * (1.0 + (ag - 1.0) * ka)
            bc = kk * ag
            log_p = _dot_split_lhs_rhs(tri, ld)
            log_p_last = log_p[last:last + 1]
            dec_out = jnp.exp(-log_p)
            dec_tail = jnp.exp(log_p_last - log_p)
            wide[di, wi] = dict(
                rows=rows, incl=incl, strict=incl & (row != col), vc=v_ref[0, rows].astype(F32),
                ar=jnp.concatenate([-kk * jnp.exp(log_p - ld), r_ref[0, rows] * jnp.exp(log_p)], axis=0),
                b_out=bc * dec_out, k_out=kc * dec_out,
                bk=jnp.concatenate([bc * dec_tail, kc * dec_tail], axis=0),
                state_decay=jnp.exp(log_p_last))

    chains = [(di, wi, gi) for wi in range(n_sub) for di in range(2) for gi in range(n_groups)]
    w = lambda ch, name: wide[ch[0], ch[1]][name]
    part = lambda ch, name: wide[ch[0], ch[1]][name][:, cols[ch[2]]]
    xb = [_dot_nt(part(ch, 'ar'), stack(part(ch, 'b_out'))) for ch in chains]
    xk = [_dot_nt(part(ch, 'ar'), stack(part(ch, 'k_out'))) for ch in chains]
    a_ab = [jnp.where(w(ch, 'strict'), x[:CHUNK], 0.0) for ch, x in zip(chains, xb)]
    a_rb = [jnp.where(w(ch, 'incl'), x[CHUNK:], 0.0) for ch, x in zip(chains, xb)]
    a_kk = [jnp.concatenate([jnp.where(w(ch, 'strict'), x[:CHUNK], 0.0), jnp.where(w(ch, 'incl'), x[CHUNK:], 0.0)],
                            axis=0) for ch, x in zip(chains, xk)]
    av = [_dot(a, stack(part(ch, 'vc'))) for ch, a in zip(chains, a_kk)]

    m = [eye + jnp.where(row // 2 == col // 2, a, 0.0) for a in a_ab]
    s = 2
    while s < CHUNK:
        couples = (row // (2 * s) == col // (2 * s)) & (row // s != col // s)
        t_off = [_dot(jnp.where(couples, a, 0.0), stack(mm)) for a, mm in zip(a_ab, m)]
        m = [mm + _dot(mm, stack(t)) for mm, t in zip(m, t_off)]
        s *= 2

    state = {(di, gi): state_ref[di * n_groups + gi] for di in range(2) for gi in range(n_groups)}
    for wi in range(n_sub):
        idx = [i for i, ch in enumerate(chains) if ch[1] == wi]
        arh = {i: _dot_nt(part(chains[i], 'ar'), state[chains[i][0], chains[i][2]]) for i in idx}
        u = {i: _dot(m[i], stack(arh[i][:CHUNK] + av[i][:CHUNK])) for i in idx}
        y = {i: arh[i][CHUNK:] + _dot(a_rb[i], stack(u[i])) + av[i][CHUNK:] for i in idx}
        upd = {i: _dot_tn(jnp.concatenate([u[i], part(chains[i], 'vc')], axis=0), part(chains[i], 'bk')) for i in idx}
        for di, y_ref in ((0, yf_ref), (1, yb_ref)):
            y_ref[0, wide[di, wi]['rows']] = jnp.concatenate([y[i] for i in idx if chains[i][0] == di],
                                                             axis=1).astype(y_ref.dtype)
        for i in idx:
            di, _, gi = chains[i]
            state[di, gi] = part(chains[i], 'state_decay') * state[di, gi] + jnp.where(block_diag, upd[i], 0.0)
    for (di, gi), s in state.items():
        state_ref[di * n_groups + gi] = s


def _dot_split_lhs_rhs(tri_bf16, x):
    hi, lo = _split(x)
    return (jnp.dot(tri_bf16, hi, preferred_element_type=F32)
            + jnp.dot(tri_bf16, lo, preferred_element_type=F32))


def _wkv(r, v, k, kk, a, ld, k_a, ctx_len):
    b, t, d = r.shape
    rows = WKV_STEP_CHUNKS * CHUNK
    assert ctx_len % rows == 0 and t % rows == 0
    n_steps = t // rows
    n_ctx = ctx_len // rows

    def back(ci):
        return jnp.where(ci < n_ctx, n_ctx - 1 - ci, n_steps - 1 - (ci - n_ctx))

    fwd = pl.BlockSpec((1, rows, d), lambda bi, ci: (bi, ci, 0))
    bwd = pl.BlockSpec((1, rows, d), lambda bi, ci: (bi, back(ci), 0))
    fwd2 = pl.BlockSpec((1, 1, rows, d), lambda bi, ci: (0, bi, ci, 0))
    bwd2 = pl.BlockSpec((1, 1, rows, d), lambda bi, ci: (1, bi, back(ci), 0))
    gw = HEADS_PER_GROUP * RWKV_HEAD
    y_shape = jax.ShapeDtypeStruct((b, t, d), BF16)
    return pl.pallas_call(
        _wkv_kernel,
        grid=(b, n_steps),
        in_specs=[fwd, fwd, fwd, fwd, fwd2, fwd2, bwd, bwd, bwd, bwd, bwd2, bwd2, _const_spec((1, d))],
        out_specs=[fwd, bwd],
        out_shape=[y_shape, y_shape],
        scratch_shapes=[pltpu.VMEM((2 * d // gw, gw, gw), F32)],
        compiler_params=_cparams(("arbitrary", "arbitrary")),
    )(r, v, k, kk, a, ld, r, v, k, kk, a, ld, k_a[None])


def _rwkv_out_kernel(x_ref, yf_ref, yb_ref, bv_ref, g_ref, mod_ref, gnw_ref, gnb_ref, gmean_ref, wo_ref, out_ref):
    y = yf_ref[0].astype(F32) + yb_ref[0].astype(F32)
    gmean = gmean_ref[...]
    dev = y - _head_sum(y, gmean)
    var = _head_sum(dev * dev, gmean)
    y_n = dev * lax.rsqrt(var + GN_EPS) * gnw_ref[...] + gnb_ref[...]
    z = (y_n + bv_ref[0]) * g_ref[0]
    out_ref[0] = x_ref[0] + mod_ref[0][2:3] * _dot(z, wo_ref[...])


def _rwkv_out(xa, y, bv, g, mods, gn_w, gn_b, w_o, n_ctx_tiles):
    b, t, d = xa.shape
    tm = TOKEN_TILE
    s = t - n_ctx_tiles * tm
    gmean = _head_indicator(RWKV_HEAD, 1.0 / RWKV_HEAD)
    lat = pl.BlockSpec((1, tm, d), lambda bi, ti: (bi, ti + n_ctx_tiles, 0))
    return pl.pallas_call(
        _rwkv_out_kernel,
        grid=(b, s // tm),
        in_specs=[
            lat, lat, lat, lat, lat,
            pl.BlockSpec((1, N_MOD, d), lambda bi, ti: (bi, 0, 0)),
            _const_spec((1, d)), _const_spec((1, d)), _const_spec((MXU_WIDTH, MXU_WIDTH)), _const_spec((d, d)),
        ],
        out_specs=pl.BlockSpec((1, tm, d), lambda bi, ti: (bi, ti, 0)),
        out_shape=jax.ShapeDtypeStruct((b, s, d), F32),
        compiler_params=_cparams(("parallel", "parallel")),
    )(xa, y[0], y[1], bv, g, mods, gn_w[None], gn_b[None], gmean, w_o.astype(BF16))


def _rope_tables(seq, ctx_len):
    rows = seq // GRID_W
    row = jnp.repeat(jnp.arange(rows, dtype=F32), GRID_W)
    col = jnp.tile(jnp.arange(GRID_W, dtype=F32), rows)
    n_freq = HEAD_DIM // 4
    inv_freq = ROPE_THETA ** (-jnp.arange(n_freq, dtype=F32) / n_freq)
    ang = jnp.concatenate([row[:, None] * inv_freq, col[:, None] * inv_freq], axis=-1)
    cos = jnp.repeat(jnp.cos(ang), 2, axis=1)
    sin = jnp.repeat(jnp.sin(ang), 2, axis=1) * jnp.tile(jnp.array([-1.0, 1.0], F32), HEAD_DIM // 2)
    cos = jnp.concatenate([jnp.ones((ctx_len, HEAD_DIM), F32), cos], axis=0)
    sin = jnp.concatenate([jnp.zeros((ctx_len, HEAD_DIM), F32), sin], axis=0)
    reps = LANES // HEAD_DIM
    return jnp.tile(cos, (1, reps)), jnp.tile(sin, (1, reps))


def kernel(x, c, ctx, c_ctx, ada_w, ada_b, attn_w_qkv, attn_q_gain, attn_k_gain, attn_w_o, rwkv_mu, rwkv_w_r, rwkv_w_k, rwkv_w_v, rwkv_w_o, rwkv_decay_w0, rwkv_decay_w1, rwkv_decay_w2, rwkv_aaa_a0, rwkv_aaa_a1, rwkv_aaa_a2, rwkv_gate_g1, rwkv_gate_g2, rwkv_k_k, rwkv_k_a, rwkv_r_k, rwkv_gn_w, rwkv_gn_b, ffn_w_up, ffn_conv_w, ffn_conv_b, ffn_w_down, final_gain):
    b, seq, d = x.shape
    ctx_len = ctx.shape[1]
    assert ada_w.shape[0] == 2 and attn_w_qkv.shape[0] == 1 and rwkv_w_r.shape[0] == 1
    assert b + 1 <= MOD_ROWS and ctx_len % TOKEN_TILE == 0 and seq % TOKEN_TILE == 0 and seq % GRID_W == 0
    n_ctx_tiles = ctx_len // TOKEN_TILE

    cc = jnp.zeros((MOD_ROWS, d), F32).at[:b].set(c).at[b].set(c_ctx)
    mods = _ada_mods(cc, ada_w, ada_b)

    cos_t, sin_t = _rope_tables(seq, ctx_len)
    q, kt, v = _attn_pre(ctx, x, mods[0], attn_w_qkv[0], attn_q_gain[0], attn_k_gain[0], cos_t, sin_t, n_ctx_tiles)
    o = _flash(q, kt, v, ctx_len)
    xa = _attn_out(ctx, x, o, mods[0], attn_w_o[0], n_ctx_tiles)
    xa = _conv_ffn(xa, mods[0], ffn_w_up[0], ffn_conv_w[0], ffn_conv_b[0], ffn_w_down[0], n_ctx_tiles, TOKEN_TILE)

    p = dict(mu=rwkv_mu[0], w_r=rwkv_w_r[0], w_k=rwkv_w_k[0], w_v=rwkv_w_v[0],
             decay_w0=rwkv_decay_w0[0], decay_w1=rwkv_decay_w1[0], decay_w2=rwkv_decay_w2[0],
             aaa_a0=rwkv_aaa_a0[0], aaa_a1=rwkv_aaa_a1[0], aaa_a2=rwkv_aaa_a2[0],
             gate_g1=rwkv_gate_g1[0], gate_g2=rwkv_gate_g2[0],
             k_k=rwkv_k_k[0], k_a=rwkv_k_a[0], r_k=rwkv_r_k[0])
    r, vv, k, kk, a, ld, g, bv = _rwkv_pre(xa, mods[1], p, n_ctx_tiles)
    y = _wkv(r, vv, k, kk, a, ld, rwkv_k_a[0], ctx_len)
    xl = _rwkv_out(xa, y, bv, g, mods[1], rwkv_gn_w[0], rwkv_gn_b[0], rwkv_w_o[0], n_ctx_tiles)
    return _conv_ffn(xl, mods[1], ffn_w_up[1], ffn_conv_w[1], ffn_conv_b[1], ffn_w_down[1], 0, TOKEN_TILE,
                     final_gain=final_gain)
```

```python
import functools
import math

import jax
import jax.numpy as jnp
from jax import lax
from jax.experimental import pallas as pl
from jax.experimental.pallas import tpu as pltpu

F32 = jnp.float32
BF16 = jnp.bfloat16

LANES = 128
SUBLANES = 8
MXU_WIDTH = 256
VMEM_LIMIT_BYTES = 56 * 1024 * 1024

N_HEADS = 16
N_KV_HEADS = 4
HEAD_DIM = 64
GROUP = N_HEADS // N_KV_HEADS
RWKV_HEAD = 64
GRID_W = 64
ROPE_THETA = 10000.0
NORM_EPS = 1e-6
GN_EPS = 64e-5
N_MOD = 6
MOD_ROWS = 8

TOKEN_TILE = 256
RWKV_TILE = 256
Q_TILE = 256
KV_TILE = 2816
KV_HEADS_PER_STEP = 1
FF_CHUNK = 256
CHUNK = 64
HEADS_PER_GROUP = 4
WKV_STEP_CHUNKS = 2
HALO = SUBLANES


def _cparams(sem):
    return pltpu.CompilerParams(dimension_semantics=sem, vmem_limit_bytes=VMEM_LIMIT_BYTES)


def _dot(a, b):
    return jnp.dot(a.astype(BF16), b.astype(BF16), preferred_element_type=F32)


def _dot_nt(a, b):
    return lax.dot_general(a.astype(BF16), b.astype(BF16), (((1,), (1,)), ((), ())),
                           preferred_element_type=F32)


def _dot_tn(a, b):
    return lax.dot_general(a.astype(BF16), b.astype(BF16), (((0,), (0,)), ((), ())),
                           preferred_element_type=F32)


def _split(a):
    hi = a.astype(BF16)
    lo = (a - hi.astype(F32)).astype(BF16)
    return hi, lo


def _head_indicator(head_dim, scale):
    return jnp.kron(jnp.eye(MXU_WIDTH // head_dim, dtype=F32), jnp.full((head_dim, head_dim), scale, F32)).astype(BF16)


def _head_sum(a, indicator):
    a = a.astype(BF16)
    blocks = [jnp.dot(a[:, j:j + MXU_WIDTH], indicator, preferred_element_type=F32)
              for j in range(0, a.shape[1], MXU_WIDTH)]
    return blocks[0] if len(blocks) == 1 else jnp.concatenate(blocks, axis=1)


def _dot_split_both(a, b):
    ah, al = _split(a)
    bh, bl = _split(b)
    d = functools.partial(jnp.dot, preferred_element_type=F32)
    return d(ah, bh) + (d(ah, bl) + d(al, bh))


def _sigmoid(x):
    return 1.0 / (1.0 + jnp.exp(-x))


def _rms_norm(x):
    return x * lax.rsqrt(jnp.mean(x * x, axis=-1, keepdims=True) + NORM_EPS)


def _modulate(x, shift, scale):
    return _rms_norm(x) * (1.0 + scale) + shift


def _const_spec(shape):
    nd = len(shape)
    return pl.BlockSpec(shape, lambda *_: (0,) * nd)


def _ada_kernel(c_ref, w_ref, b_ref, o_ref):
    cc = c_ref[...]
    o_ref[0, 0] = _dot_split_both(cc * _sigmoid(cc), w_ref[0]) + b_ref[0, 0]


def _ada_mods(cc, ada_w, ada_b):
    depth, d, _ = ada_w.shape
    out = pl.pallas_call(
        _ada_kernel,
        grid=(depth, N_MOD),
        in_specs=[
            pl.BlockSpec((MOD_ROWS, d), lambda i, n: (0, 0)),
            pl.BlockSpec((1, d, d), lambda i, n: (i, 0, n)),
            pl.BlockSpec((1, 1, 1, d), lambda i, n: (i, n, 0, 0)),
        ],
        out_specs=pl.BlockSpec((1, 1, MOD_ROWS, d), lambda i, n: (i, n, 0, 0)),
        out_shape=jax.ShapeDtypeStruct((depth, N_MOD, MOD_ROWS, d), F32),
        compiler_params=_cparams(("arbitrary", "arbitrary")),
    )(cc, ada_w, ada_b.reshape(depth, N_MOD, 1, d))
    return out.transpose(0, 2, 1, 3)


def _rope(x, cos, sin):
    width = x.shape[1]
    reps = width // LANES
    cc = jnp.concatenate([cos] * reps, axis=1)
    ss = jnp.concatenate([sin] * reps, axis=1)
    lane = lax.broadcasted_iota(jnp.int32, x.shape, 1)
    partner = jnp.where((lane & 1) == 0, pltpu.roll(x, width - 1, 1), pltpu.roll(x, 1, 1))
    return x * cc + partner * ss


def _token_tile_specs(tm, d, n_ctx_tiles):
    return [pl.BlockSpec((1, tm, d), lambda bi, ti: (bi, jnp.minimum(ti, n_ctx_tiles - 1), 0)),
            pl.BlockSpec((1, tm, d), lambda bi, ti: (bi, jnp.maximum(ti - n_ctx_tiles, 0), 0))]


def _token_tile(c_ref, x_ref, n_ctx_tiles):
    return jnp.where(pl.program_id(1) < n_ctx_tiles, c_ref[0], x_ref[0])


def _attn_pre_kernel(c_ref, x_ref, mod_ref, w_ref, qg_ref, kg_ref, hmean_ref, cos_ref, sin_ref,
                     q_ref, kt_ref, v_ref, *, q_dim, kv_dim, n_ctx_tiles):
    mod = mod_ref[0]
    h = _modulate(_token_tile(c_ref, x_ref, n_ctx_tiles), mod[0:1], mod[1:2])
    qkv = _dot(h, w_ref[...])
    q = qkv[:, :q_dim]
    k = qkv[:, q_dim:q_dim + kv_dim]
    v = qkv[:, q_dim + kv_dim:]
    hmean = hmean_ref[...]
    q = q * lax.rsqrt(_head_sum(q * q, hmean) + NORM_EPS) * qg_ref[...]
    k = k * lax.rsqrt(_head_sum(k * k, hmean) + NORM_EPS) * kg_ref[...]
    cos = cos_ref[...]
    sin = sin_ref[...]
    q = _rope(q, cos, sin) * (HEAD_DIM ** -0.5 * math.log2(math.e))
    k = _rope(k, cos, sin)
    q_ref[0] = q.astype(BF16)
    kt_ref[0] = k.T.astype(BF16)
    rows = v.shape[0]
    one_col = (lax.broadcasted_iota(jnp.int32, (rows, HEAD_DIM), 1) == 0).astype(F32)
    for g in range(N_KV_HEADS):
        v_ref[0, g] = jnp.concatenate([v[:, g * HEAD_DIM:(g + 1) * HEAD_DIM], one_col], axis=1).astype(BF16)


def _attn_pre(ctx, x, mods, w_qkv, q_gain, k_gain, cos_t, sin_t, n_ctx_tiles):
    b, seq, d = x.shape
    t = ctx.shape[1] + seq
    q_dim = N_HEADS * HEAD_DIM
    kv_dim = N_KV_HEADS * HEAD_DIM
    tm = TOKEN_TILE
    nb = b
    mod_map = lambda bi, ti: (jnp.where(ti < n_ctx_tiles, nb, bi), 0, 0)
    return pl.pallas_call(
        functools.partial(_attn_pre_kernel, q_dim=q_dim, kv_dim=kv_dim, n_ctx_tiles=n_ctx_tiles),
        grid=(b, t // tm),
        in_specs=_token_tile_specs(tm, d, n_ctx_tiles) + [
            pl.BlockSpec((1, N_MOD, d), mod_map),
            _const_spec((d, q_dim + 2 * kv_dim)),
            _const_spec((1, q_dim)),
            _const_spec((1, kv_dim)),
            _const_spec((MXU_WIDTH, MXU_WIDTH)),
            pl.BlockSpec((tm, LANES), lambda bi, ti: (ti, 0)),
            pl.BlockSpec((tm, LANES), lambda bi, ti: (ti, 0)),
        ],
        out_specs=[
            pl.BlockSpec((1, tm, q_dim), lambda bi, ti: (bi, ti, 0)),
            pl.BlockSpec((1, kv_dim, tm), lambda bi, ti: (bi, 0, ti)),
            pl.BlockSpec((1, N_KV_HEADS, tm, LANES), lambda bi, ti: (bi, 0, ti, 0)),
        ],
        out_shape=[
            jax.ShapeDtypeStruct((b, t, q_dim), BF16),
            jax.ShapeDtypeStruct((b, kv_dim, t), BF16),
            jax.ShapeDtypeStruct((b, N_KV_HEADS, t, LANES), BF16),
        ],
        compiler_params=_cparams(("parallel", "parallel")),
    )(ctx, x, mods, w_qkv.astype(BF16), jnp.tile(q_gain, N_HEADS)[None], jnp.tile(k_gain, N_KV_HEADS)[None],
      _head_indicator(HEAD_DIM, 1.0 / HEAD_DIM), cos_t, sin_t)


def _flash_kernel(q_ref, kt_ref, v_ref, o_ref, q4_ref, *, n_ctx_q, ctx_len, kv_tiles):
    qi = pl.program_id(2)
    tq = q_ref.shape[1]
    n_heads = q_ref.shape[2] // HEAD_DIM
    q = q_ref[0]
    for h in range(n_heads):
        q4_ref[h * tq:(h + 1) * tq] = q[:, h * HEAD_DIM:(h + 1) * HEAD_DIM]

    def attend(tiles):
        steps = [(start, size, g * GROUP + j) for g in range(n_heads // GROUP) for start, size in tiles
                 for j in range(GROUP)]

        def scores(st):
            start, size, h = st
            g = h // GROUP
            return jnp.dot(q4_ref[h * tq:(h + 1) * tq], kt_ref[0, g * HEAD_DIM:(g + 1) * HEAD_DIM, start:start + size],
                           preferred_element_type=F32)

        m = [None] * n_heads
        acc = [None] * n_heads
        s_next = scores(steps[0])
        for i, (start, size, h) in enumerate(steps):
            s = s_next
            if i + 1 < len(steps):
                s_next = scores(steps[i + 1])
            cols = [s[:, t * LANES:(t + 1) * LANES] for t in range(size // LANES)]
            m_cur = jnp.max(functools.reduce(jnp.maximum, cols), axis=1, keepdims=True)
            m_new = jnp.broadcast_to(m_cur, (tq, LANES)) if m[h] is None else jnp.maximum(m[h], m_cur)
            p = jnp.concatenate([jnp.exp2(c - m_new) for c in cols], axis=1).astype(BF16)
            pv = jnp.dot(p, v_ref[0, h // GROUP, start:start + size, :], preferred_element_type=F32)
            acc[h] = pv if acc[h] is None else jnp.exp2(m[h] - m_new) * acc[h] + pv
            m[h] = m_new
        o = [a[:, :HEAD_DIM] / a[:, HEAD_DIM:HEAD_DIM + 1] for a in acc]
        o_ref[0] = jnp.concatenate(o, axis=1).astype(BF16)

    @pl.when(qi < n_ctx_q)
    def _():
        attend([(0, ctx_len)])

    @pl.when(qi >= n_ctx_q)
    def _():
        attend(kv_tiles)


def _flash(q, kt, v, ctx_len):
    b, t, q_dim = q.shape
    tq = Q_TILE
    kvh = KV_HEADS_PER_STEP
    gw = kvh * GROUP * HEAD_DIM
    n_kv = next(n for n in range(1, t // LANES + 1) if t % (n * LANES) == 0 and t // n <= KV_TILE)
    kv_tiles = [(i * (t // n_kv), t // n_kv) for i in range(n_kv)]
    kern = functools.partial(_flash_kernel, n_ctx_q=ctx_len // tq, ctx_len=ctx_len, kv_tiles=kv_tiles)
    return pl.pallas_call(
        kern,
        grid=(b, N_KV_HEADS // kvh, t // tq),
        in_specs=[
            pl.BlockSpec((1, tq, gw), lambda bi, g, qi: (bi, qi, g)),
            pl.BlockSpec((1, kvh * HEAD_DIM, t), lambda bi, g, qi: (bi, g, 0)),
            pl.BlockSpec((1, kvh, t, LANES), lambda bi, g, qi: (bi, g, 0, 0)),
        ],
        out_specs=pl.BlockSpec((1, tq, gw), lambda bi, g, qi: (bi, qi, g)),
        out_shape=jax.ShapeDtypeStruct((b, t, q_dim), BF16),
        scratch_shapes=[pltpu.VMEM((kvh * GROUP * tq, HEAD_DIM), BF16)],
        compiler_params=_cparams(("parallel", "parallel", "arbitrary")),
    )(q, kt, v)


def _attn_out_kernel(c_ref, x_ref, o_ref, mod_ref, w_ref, out_ref, *, n_ctx_tiles):
    y = jnp.dot(o_ref[0], w_ref[...], preferred_element_type=F32)
    out_ref[0] = _token_tile(c_ref, x_ref, n_ctx_tiles) + mod_ref[0][2:3] * y


def _attn_out(ctx, x, o, mods, w_o, n_ctx_tiles):
    b, seq, d = x.shape
    t = ctx.shape[1] + seq
    tm = TOKEN_TILE
    nb = b
    mod_map = lambda bi, ti: (jnp.where(ti < n_ctx_tiles, nb, bi), 0, 0)
    return pl.pallas_call(
        functools.partial(_attn_out_kernel, n_ctx_tiles=n_ctx_tiles),
        grid=(b, t // tm),
        in_specs=_token_tile_specs(tm, d, n_ctx_tiles) + [
            pl.BlockSpec((1, tm, o.shape[2]), lambda bi, ti: (bi, ti, 0)),
            pl.BlockSpec((1, N_MOD, d), mod_map),
            _const_spec(w_o.shape),
        ],
        out_specs=pl.BlockSpec((1, tm, d), lambda bi, ti: (bi, ti, 0)),
        out_shape=jax.ShapeDtypeStruct((b, t, d), F32),
        compiler_params=_cparams(("parallel", "parallel")),
    )(ctx, x, o, mods, w_o.astype(BF16))


def _halo_specs(tm, d, n_row_blocks):
    per = tm // HALO
    return [
        pl.BlockSpec((1, tm, d), lambda bi, ti: (bi, ti, 0)),
        pl.BlockSpec((1, HALO, d), lambda bi, ti: (bi, jnp.maximum(ti * per - 1, 0), 0)),
        pl.BlockSpec((1, HALO, d), lambda bi, ti: (bi, jnp.minimum((ti + 1) * per, n_row_blocks - 1), 0)),
    ]


def _edge_keep(ti, n_ctx_tiles, n_tiles):
    first = (ti == 0) | (ti == n_ctx_tiles)
    last = (ti == n_tiles - 1) | (ti == n_ctx_tiles - 1)
    return jnp.where(first, 0.0, 1.0).astype(F32), jnp.where(last, 0.0, 1.0).astype(F32)


def _ffn_kernel(*refs, n_ctx_tiles, n_tiles, n_fc, final):
    if final:
        xm_ref, xp_ref, xn_ref, mod_ref, wup_ref, cw_ref, cb_ref, wdn_ref, fg_ref, out_ref = refs
    else:
        xm_ref, xp_ref, xn_ref, mod_ref, wup_ref, cw_ref, cb_ref, wdn_ref, out_ref = refs
    tm = xm_ref.shape[1]
    ext = tm + 2 * HALO
    mod = mod_ref[0]
    shift, scale, gate = mod[3:4], mod[4:5], mod[5:6]
    keep_prev, keep_next = _edge_keep(pl.program_id(1), n_ctx_tiles, n_tiles)
    xm = xm_ref[0]
    h = jnp.concatenate([_modulate(xp_ref[0], shift, scale) * keep_prev,
                         _modulate(xm, shift, scale),
                         _modulate(xn_ref[0], shift, scale) * keep_next], axis=0).astype(BF16)

    def conv(u, cw, cb):
        y = cw[0:1] * pltpu.roll(u, 1, 0) + cw[1:2] * u + cw[2:3] * pltpu.roll(u, ext - 1, 0) + cb
        return y[HALO:HALO + tm]

    col = lambda j: slice(j * FF_CHUNK, (j + 1) * FF_CHUNK)
    up = lambda j: (jnp.dot(h, wup_ref[:, col(j)], preferred_element_type=F32),
                    jnp.dot(h, wup_ref[:, col(n_fc + j)], preferred_element_type=F32))
    acc = None
    ahead = [up(0), up(1)]
    for j in range(n_fc):
        ug, uv = ahead.pop(0)
        if j + 2 < n_fc:
            ahead.append(up(j + 2))
        cg = conv(ug, cw_ref[:, col(j)], cb_ref[:, col(j)])
        cv = conv(uv, cw_ref[:, col(n_fc + j)], cb_ref[:, col(n_fc + j)])
        act = cg * _sigmoid(cg) * cv
        down = jnp.dot(act.astype(BF16), wdn_ref[col(j), :], preferred_element_type=F32)
        acc = down if acc is None else acc + down

    out = xm + gate * acc
    if final:
        out = _rms_norm(out) * fg_ref[...]
    out_ref[0] = out


def _conv_ffn(xa, mods, w_up, conv_w, conv_b, w_down, n_ctx_tiles, tm, final_gain=None):
    b, t, d = xa.shape
    d_ff = w_down.shape[0]
    n_fc = d_ff // FF_CHUNK
    n_tiles = t // tm
    nb = b
    final = final_gain is not None
    wup = w_up.astype(BF16)
    cw = conv_w
    cb = conv_b[None]
    wdn = w_down.astype(BF16)
    mod_map = lambda bi, ti: (jnp.where(ti < n_ctx_tiles, nb, bi), 0, 0)
    in_specs = _halo_specs(tm, d, t // HALO) + [
        pl.BlockSpec((1, N_MOD, d), mod_map),
        _const_spec(wup.shape), _const_spec(cw.shape), _const_spec(cb.shape), _const_spec(wdn.shape),
    ]
    args = [xa, xa, xa, mods, wup, cw, cb, wdn]
    if final:
        in_specs.append(_const_spec((1, d)))
        args.append(final_gain[None])
    return pl.pallas_call(
        functools.partial(_ffn_kernel, n_ctx_tiles=n_ctx_tiles, n_tiles=n_tiles, n_fc=n_fc, final=final),
        grid=(b, n_tiles),
        in_specs=in_specs,
        out_specs=pl.BlockSpec((1, tm, d), lambda bi, ti: (bi, ti, 0)),
        out_shape=jax.ShapeDtypeStruct((b, t, d), F32),
        compiler_params=_cparams(("parallel", "parallel")),
    )(*args)


def _rwkv_pre_kernel(xm_ref, xp_ref, xn_ref, mod_ref, mu_ref, wr_ref, wk_ref, wv_ref,
                     dw1_ref, dw2_ref, w0_ref, a1_ref, a2_ref, a0_ref, g1_ref, g2_ref,
                     kk_ref, ka_ref, rk_ref, gsum_ref,
                     r_out, v_out, k_out, kk_out, a_out, ld_out, g_out, bv_out,
                     *, n_ctx_tiles, n_tiles):
    tm = xm_ref.shape[1]
    d = xm_ref.shape[2]
    ext = tm + 2 * HALO
    mod = mod_ref[0]
    shift, scale = mod[0:1], mod[1:2]
    keep_prev, keep_next = _edge_keep(pl.program_id(1), n_ctx_tiles, n_tiles)
    h = _modulate(xm_ref[0], shift, scale)
    he = jnp.concatenate([_modulate(xp_ref[0], shift, scale) * keep_prev, h,
                          _modulate(xn_ref[0], shift, scale) * keep_next], axis=0)
    xx = 0.5 * (pltpu.roll(he, 1, 0) + pltpu.roll(he, ext - 1, 0))[HALO:HALO + tm] - h
    mu = mu_ref[...]
    mix = lambda i: h + xx * mu[i:i + 1]
    r = _dot(mix(0), wr_ref[...])
    k = _dot(mix(2), wk_ref[...])
    v = _dot(mix(3), wv_ref[...])
    z = w0_ref[...] + _dot(jnp.tanh(_dot(mix(1), dw1_ref[...])), dw2_ref[...])
    log_decay = -math.exp(-0.5) * _sigmoid(z)
    a = _sigmoid(a0_ref[...] + _dot(_dot(mix(4), a1_ref[...]), a2_ref[...]))
    g = _dot(_sigmoid(_dot(mix(5), g1_ref[...])), g2_ref[...])
    gsum = gsum_ref[...]
    kx = k * kk_ref[...]
    kk = kx / jnp.maximum(jnp.sqrt(_head_sum(kx * kx, gsum)), 1e-12)
    ka = ka_ref[...]
    bonus = (r * rk_ref[...]) * k * (2.0 + (a[:, :d] + a[:, d:] - 2.0) * ka)
    r_out[0] = r
    v_out[0] = v.astype(BF16)
    k_out[0] = k
    kk_out[0] = kk
    a_out[0, 0] = a[:, :d]
    a_out[1, 0] = a[:, d:]
    ld_out[0, 0] = log_decay[:, :d]
    ld_out[1, 0] = log_decay[:, d:]
    g_out[0] = g.astype(BF16)
    bv_out[0] = (_head_sum(bonus, gsum) * v).astype(BF16)


def _block_diag2(m0, m1):
    z = jnp.zeros_like(m0)
    return jnp.concatenate([jnp.concatenate([m0, z], axis=1), jnp.concatenate([z, m1], axis=1)], axis=0)


def _rwkv_pre(xa, mods, p, n_ctx_tiles):
    b, t, d = xa.shape
    tm = RWKV_TILE
    n_tiles = t // tm
    nb = b
    n_ctx = n_ctx_tiles * (TOKEN_TILE // tm)
    mod_map = lambda bi, ti: (jnp.where(ti < n_ctx, nb, bi), 0, 0)
    gate_pad = -p['gate_g1'].shape[1] % LANES
    consts = [
        p['mu'],
        p['w_r'].astype(BF16), p['w_k'].astype(BF16), p['w_v'].astype(BF16),
        jnp.concatenate([p['decay_w1'][0], p['decay_w1'][1]], axis=1).astype(BF16),
        _block_diag2(p['decay_w2'][0], p['decay_w2'][1]).astype(BF16),
        p['decay_w0'].reshape(1, 2 * d),
        jnp.concatenate([p['aaa_a1'][0], p['aaa_a1'][1]], axis=1).astype(BF16),
        _block_diag2(p['aaa_a2'][0], p['aaa_a2'][1]).astype(BF16),
        p['aaa_a0'].reshape(1, 2 * d),
        jnp.pad(p['gate_g1'], ((0, 0), (0, gate_pad))).astype(BF16),
        jnp.pad(p['gate_g2'], ((0, gate_pad), (0, 0))).astype(BF16),
        p['k_k'][None], p['k_a'][None], p['r_k'].reshape(1, d),
        _head_indicator(RWKV_HEAD, 1.0),
    ]
    tok = pl.BlockSpec((1, tm, d), lambda bi, ti: (bi, ti, 0))
    tok2 = pl.BlockSpec((2, 1, tm, d), lambda bi, ti: (0, bi, ti, 0))
    s1 = jax.ShapeDtypeStruct((b, t, d), F32)
    s2 = jax.ShapeDtypeStruct((2, b, t, d), F32)
    h1 = jax.ShapeDtypeStruct((b, t, d), BF16)
    return pl.pallas_call(
        functools.partial(_rwkv_pre_kernel, n_ctx_tiles=n_ctx, n_tiles=n_tiles),
        grid=(b, n_tiles),
        in_specs=_halo_specs(tm, d, t // HALO) + [pl.BlockSpec((1, N_MOD, d), mod_map)]
        + [_const_spec(c.shape) for c in consts],
        out_specs=[tok, tok, tok, tok, tok2, tok2, tok, tok],
        out_shape=[s1, h1, s1, s1, s2, s2, h1, h1],
        compiler_params=_cparams(("parallel", "parallel")),
    )(xa, xa, xa, mods, *consts)


def _head_stack(x, lane_head):
    x = x.astype(BF16)
    return jnp.concatenate([x * (lane_head == hh).astype(BF16) for hh in range(HEADS_PER_GROUP)], axis=0)


def _wkv_kernel(rf_ref, vf_ref, kf_ref, kkf_ref, af_ref, ldf_ref, rb_ref, vb_ref, kb_ref, kkb_ref, ab_ref, ldb_ref,
                ka_ref, yf_ref, yb_ref, state_ref):
    gw = HEADS_PER_GROUP * RWKV_HEAD
    n_groups = rf_ref.shape[2] // gw

    @pl.when(pl.program_id(1) == 0)
    def _():
        state_ref[...] = jnp.zeros(state_ref.shape, F32)

    row = lax.broadcasted_iota(jnp.int32, (CHUNK, gw), 0)
    lane = lax.broadcasted_iota(jnp.int32, (CHUNK, gw), 1)
    col = lane % CHUNK
    lane_head = lane // RWKV_HEAD
    eye = (row == col).astype(F32)
    bd_r = lax.broadcasted_iota(jnp.int32, (gw, gw), 0) // RWKV_HEAD
    bd_c = lax.broadcasted_iota(jnp.int32, (gw, gw), 1) // RWKV_HEAD
    block_diag = bd_r == bd_c
    ka = ka_ref[...]
    stack = lambda x: _head_stack(x, lane_head)
    cols = [slice(gi * gw, (gi + 1) * gw) for gi in range(n_groups)]

    n_sub = rf_ref.shape[1] // CHUNK
    wide = {}
    for di, (refs, incl, last) in enumerate((((rf_ref, vf_ref, kf_ref, kkf_ref, af_ref, ldf_ref), row >= col, CHUNK - 1),
                                             ((rb_ref, vb_ref, kb_ref, kkb_ref, ab_ref, ldb_ref), row <= col, 0))):
        r_ref, v_ref, k_ref, kk_ref, a_ref, ld_ref = refs
        tri = jnp.where(incl[:, :CHUNK], 1.0, 0.0).astype(BF16)
        for wi in range(n_sub):
            ci = wi if di == 0 else n_sub - 1 - wi
            rows = slice(ci * CHUNK, (ci + 1) * CHUNK)
            ld = ld_ref[0, 0, rows]
            ag = a_ref[0, 0, rows]
            kk = kk_ref[0, rows]
            kc = k_ref[0, rows] * (1.0 + (ag - 1.0) * ka)
            bc = kk * ag
            log_p = _dot_split_lhs_rhs(tri, ld)
            log_p_last = log_p[last:last + 1]
            dec_out = jnp.exp(-log_p)
            dec_tail = jnp.exp(log_p_last - log_p)
            wide[di, wi] = dict(
                rows=rows, incl=incl, strict=incl & (row != col), vc=v_ref[0, rows].astype(F32),
                ar=jnp.concatenate([-kk * jnp.exp(log_p - ld), r_ref[0, rows] * jnp.exp(log_p)], axis=0),
                b_out=bc * dec_out, k_out=kc * dec_out,
                bk=jnp.concatenate([bc * dec_tail, kc * dec_tail], axis=0),
                state_decay=jnp.exp(log_p_last))

    chains = [(di, wi, gi) for wi in range(n_sub) for di in range(2) for gi in range(n_groups)]
    w = lambda ch, name: wide[ch[0], ch[1]][name]
    part = lambda ch, name: wide[ch[0], ch[1]][name][:, cols[ch[2]]]
    xb = [_dot_nt(part(ch, 'ar'), stack(part(ch, 'b_out'))) for ch in chains]
    xk = [_dot_nt(part(ch, 'ar'), stack(part(ch, 'k_out'))) for ch in chains]
    a_ab = [jnp.where(w(ch, 'strict'), x[:CHUNK], 0.0) for ch, x in zip(chains, xb)]
    a_rb = [jnp.where(w(ch, 'incl'), x[CHUNK:], 0.0) for ch, x in zip(chains, xb)]
    a_kk = [jnp.concatenate([jnp.where(w(ch, 'strict'), x[:CHUNK], 0.0), jnp.where(w(ch, 'incl'), x[CHUNK:], 0.0)],
                            axis=0) for ch, x in zip(chains, xk)]
    av = [_dot(a, stack(part(ch, 'vc'))) for ch, a in zip(chains, a_kk)]

    m = [eye + jnp.where(row // 2 == col // 2, a, 0.0) for a in a_ab]
    s = 2
    while s < CHUNK:
        couples = (row // (2 * s) == col // (2 * s)) & (row // s != col // s)
        t_off = [_dot(jnp.where(couples, a, 0.0), stack(mm)) for a, mm in zip(a_ab, m)]
        m = [mm + _dot(mm, stack(t)) for mm, t in zip(m, t_off)]
        s *= 2

    state = {(di, gi): state_ref[di * n_groups + gi] for di in range(2) for gi in range(n_groups)}
    for wi in range(n_sub):
        idx = [i for i, ch in enumerate(chains) if ch[1] == wi]
        arh = {i: _dot_nt(part(chains[i], 'ar'), state[chains[i][0], chains[i][2]]) for i in idx}
        u = {i: _dot(m[i], stack(arh[i][:CHUNK] + av[i][:CHUNK])) for i in idx}
        y = {i: arh[i][CHUNK:] + _dot(a_rb[i], stack(u[i])) + av[i][CHUNK:] for i in idx}
        upd = {i: _dot_tn(jnp.concatenate([u[i], part(chains[i], 'vc')], axis=0), part(chains[i], 'bk')) for i in idx}
        for di, y_ref in ((0, yf_ref), (1, yb_ref)):
            y_ref[0, wide[di, wi]['rows']] = jnp.concatenate([y[i] for i in idx if chains[i][0] == di],
                                                             axis=1).astype(y_ref.dtype)
        for i in idx:
            di, _, gi = chains[i]
            state[di, gi] = part(chains[i], 'state_decay') * state[di, gi] + jnp.where(block_diag, upd[i], 0.0)
    for (di, gi), s in state.items():
        state_ref[di * n_groups + gi] = s


def _dot_split_lhs_rhs(tri_bf16, x):
    hi, lo = _split(x)
    return (jnp.dot(tri_bf16, hi, preferred_element_type=F32)
            + jnp.dot(tri_bf16, lo, preferred_element_type=F32))


def _wkv(r, v, k, kk, a, ld, k_a, ctx_len):
    b, t, d = r.shape
    rows = WKV_STEP_CHUNKS * CHUNK
    assert ctx_len % rows == 0 and t % rows == 0
    n_steps = t // rows
    n_ctx = ctx_len // rows

    def back(ci):
        return jnp.where(ci < n_ctx, n_ctx - 1 - ci, n_steps - 1 - (ci - n_ctx))

    fwd = pl.BlockSpec((1, rows, d), lambda bi, ci: (bi, ci, 0))
    bwd = pl.BlockSpec((1, rows, d), lambda bi, ci: (bi, back(ci), 0))
    fwd2 = pl.BlockSpec((1, 1, rows, d), lambda bi, ci: (0, bi, ci, 0))
    bwd2 = pl.BlockSpec((1, 1, rows, d), lambda bi, ci: (1, bi, back(ci), 0))
    gw = HEADS_PER_GROUP * RWKV_HEAD
    y_shape = jax.ShapeDtypeStruct((b, t, d), BF16)
    return pl.pallas_call(
        _wkv_kernel,
        grid=(b, n_steps),
        in_specs=[fwd, fwd, fwd, fwd, fwd2, fwd2, bwd, bwd, bwd, bwd, bwd2, bwd2, _const_spec((1, d))],
        out_specs=[fwd, bwd],
        out_shape=[y_shape, y_shape],
        scratch_shapes=[pltpu.VMEM((2 * d // gw, gw, gw), F32)],
        compiler_params=_cparams(("arbitrary", "arbitrary")),
    )(r, v, k, kk, a, ld, r, v, k, kk, a, ld, k_a[None])


def _rwkv_out_kernel(x_ref, yf_ref, yb_ref, bv_ref, g_ref, mod_ref, gnw_ref, gnb_ref, gmean_ref, wo_ref, out_ref):
    y = yf_ref[0].astype(F32) + yb_ref[0].astype(F32)
    gmean = gmean_ref[...]
    dev = y - _head_sum(y, gmean)
    var = _head_sum(dev * dev, gmean)
    y_n = dev * lax.rsqrt(var + GN_EPS) * gnw_ref[...] + gnb_ref[...]
    z = (y_n + bv_ref[0]) * g_ref[0]
    out_ref[0] = x_ref[0] + mod_ref[0][2:3] * _dot(z, wo_ref[...])


def _rwkv_out(xa, y, bv, g, mods, gn_w, gn_b, w_o, n_ctx_tiles):
    b, t, d = xa.shape
    tm = TOKEN_TILE
    s = t - n_ctx_tiles * tm
    gmean = _head_indicator(RWKV_HEAD, 1.0 / RWKV_HEAD)
    lat = pl.BlockSpec((1, tm, d), lambda bi, ti: (bi, ti + n_ctx_tiles, 0))
    return pl.pallas_call(
        _rwkv_out_kernel,
        grid=(b, s // tm),
        in_specs=[
            lat, lat, lat, lat, lat,
            pl.BlockSpec((1, N_MOD, d), lambda bi, ti: (bi, 0, 0)),
            _const_spec((1, d)), _const_spec((1, d)), _const_spec((MXU_WIDTH, MXU_WIDTH)), _const_spec((d, d)),
        ],
        out_specs=pl.BlockSpec((1, tm, d), lambda bi, ti: (bi, ti, 0)),
        out_shape=jax.ShapeDtypeStruct((b, s, d), F32),
        compiler_params=_cparams(("parallel", "parallel")),
    )(xa, y[0], y[1], bv, g, mods, gn_w[None], gn_b[None], gmean, w_o.astype(BF16))


def _rope_tables(seq, ctx_len):
    rows = seq // GRID_W
    row = jnp.repeat(jnp.arange(rows, dtype=F32), GRID_W)
    col = jnp.tile(jnp.arange(GRID_W, dtype=F32), rows)
    n_freq = HEAD_DIM // 4
    inv_freq = ROPE_THETA ** (-jnp.arange(n_freq, dtype=F32) / n_freq)
    ang = jnp.concatenate([row[:, None] * inv_freq, col[:, None] * inv_freq], axis=-1)
    cos = jnp.repeat(jnp.cos(ang), 2, axis=1)
    sin = jnp.repeat(jnp.sin(ang), 2, axis=1) * jnp.tile(jnp.array([-1.0, 1.0], F32), HEAD_DIM // 2)
    cos = jnp.concatenate([jnp.ones((ctx_len, HEAD_DIM), F32), cos], axis=0)
    sin = jnp.concatenate([jnp.zeros((ctx_len, HEAD_DIM), F32), sin], axis=0)
    reps = LANES // HEAD_DIM
    return jnp.tile(cos, (1, reps)), jnp.tile(sin, (1, reps))


def kernel(x, c, ctx, c_ctx, ada_w, ada_b, attn_w_qkv, attn_q_gain, attn_k_gain, attn_w_o, rwkv_mu, rwkv_w_r, rwkv_w_k, rwkv_w_v, rwkv_w_o, rwkv_decay_w0, rwkv_decay_w1, rwkv_decay_w2, rwkv_aaa_a0, rwkv_aaa_a1, rwkv_aaa_a2, rwkv_gate_g1, rwkv_gate_g2, rwkv_k_k, rwkv_k_a, rwkv_r_k, rwkv_gn_w, rwkv_gn_b, ffn_w_up, ffn_conv_w, ffn_conv_b, ffn_w_down, final_gain):
    b, seq, d = x.shape
    ctx_len = ctx.shape[1]
    assert ada_w.shape[0] == 2 and attn_w_qkv.shape[0] == 1 and rwkv_w_r.shape[0] == 1
    assert b + 1 <= MOD_ROWS and ctx_len % TOKEN_TILE == 0 and seq % TOKEN_TILE == 0 and seq % GRID_W == 0
    n_ctx_tiles = ctx_len // TOKEN_TILE

    cc = jnp.zeros((MOD_ROWS, d), F32).at[:b].set(c).at[b].set(c_ctx)
    mods = _ada_mods(cc, ada_w, ada_b)

    cos_t, sin_t = _rope_tables(seq, ctx_len)
    q, kt, v = _attn_pre(ctx, x, mods[0], attn_w_qkv[0], attn_q_gain[0], attn_k_gain[0], cos_t, sin_t, n_ctx_tiles)
    o = _flash(q, kt, v, ctx_len)
    xa = _attn_out(ctx, x, o, mods[0], attn_w_o[0], n_ctx_tiles)
    xa = _conv_ffn(xa, mods[0], ffn_w_up[0], ffn_conv_w[0], ffn_conv_b[0], ffn_w_down[0], n_ctx_tiles, TOKEN_TILE)

    p = dict(mu=rwkv_mu[0], w_r=rwkv_w_r[0], w_k=rwkv_w_k[0], w_v=rwkv_w_v[0],
             decay_w0=rwkv_decay_w0[0], decay_w1=rwkv_decay_w1[0], decay_w2=rwkv_decay_w2[0],
             aaa_a0=rwkv_aaa_a0[0], aaa_a1=rwkv_aaa_a1[0], aaa_a2=rwkv_aaa_a2[0],
             gate_g1=rwkv_gate_g1[0], gate_g2=rwkv_gate_g2[0],
             k_k=rwkv_k_k[0], k_a=rwkv_k_a[0], r_k=rwkv_r_k[0])
    r, vv, k, kk, a, ld, g, bv = _rwkv_pre(xa, mods[1], p, n_ctx_tiles)
    y = _wkv(r, vv, k, kk, a, ld, rwkv_k_a[0], ctx_len)
    xl = _rwkv_out(xa, y, bv, g, mods[1], rwkv_gn_w[0], rwkv_gn_b[0], rwkv_w_o[0], n_ctx_tiles)
    return _conv_ffn(xl, mods[1], ffn_w_up[1], ffn_conv_w[1], ffn_conv_b[1], ffn_w_down[1], 0, TOKEN_TILE,
                     final_gain=final_gain)
```

```python
import functools
import math

import jax
import jax.numpy as jnp
from jax import lax
from jax.experimental import pallas as pl
from jax.experimental.pallas import tpu as pltpu

F32 = jnp.float32
BF16 = jnp.bfloat16

LANES = 128
SUBLANES = 8
MXU_WIDTH = 256
VMEM_LIMIT_BYTES = 56 * 1024 * 1024

N_HEADS = 16
N_KV_HEADS = 4
HEAD_DIM = 64
GROUP = N_HEADS // N_KV_HEADS
RWKV_HEAD = 64
GRID_W = 64
ROPE_THETA = 10000.0
NORM_EPS = 1e-6
GN_EPS = 64e-5
N_MOD = 6
MOD_ROWS = 8

TOKEN_TILE = 256
RWKV_TILE = 256
Q_TILE = 256
KV_TILE = 2816
KV_HEADS_PER_STEP = 1
FF_CHUNK = 256
CHUNK = 64
HEADS_PER_GROUP = 4
WKV_STEP_CHUNKS = 2
HALO = SUBLANES
WIDE_HALO = 2 * SUBLANES


def _cparams(sem):
    return pltpu.CompilerParams(dimension_semantics=sem, vmem_limit_bytes=VMEM_LIMIT_BYTES)


def _dot(a, b):
    return jnp.dot(a.astype(BF16), b.astype(BF16), preferred_element_type=F32)


def _dot_nt(a, b):
    return lax.dot_general(a.astype(BF16), b.astype(BF16), (((1,), (1,)), ((), ())),
                           preferred_element_type=F32)


def _dot_tn(a, b):
    return lax.dot_general(a.astype(BF16), b.astype(BF16), (((0,), (0,)), ((), ())),
                           preferred_element_type=F32)


def _split(a):
    hi = a.astype(BF16)
    lo = (a - hi.astype(F32)).astype(BF16)
    return hi, lo


def _head_indicator(head_dim, scale):
    return jnp.kron(jnp.eye(MXU_WIDTH // head_dim, dtype=F32), jnp.full((head_dim, head_dim), scale, F32)).astype(BF16)


def _head_sum(a, indicator):
    a = a.astype(BF16)
    blocks = [jnp.dot(a[:, j:j + MXU_WIDTH], indicator, preferred_element_type=F32)
              for j in range(0, a.shape[1], MXU_WIDTH)]
    return blocks[0] if len(blocks) == 1 else jnp.concatenate(blocks, axis=1)


def _dot_split_both(a, b):
    ah, al = _split(a)
    bh, bl = _split(b)
    d = functools.partial(jnp.dot, preferred_element_type=F32)
    return d(ah, bh) + (d(ah, bl) + d(al, bh))


def _sigmoid(x):
    return 1.0 / (1.0 + jnp.exp(-x))


def _rms_norm(x):
    return x * lax.rsqrt(jnp.mean(x * x, axis=-1, keepdims=True) + NORM_EPS)


def _modulate(x, shift, scale):
    return _rms_norm(x) * (1.0 + scale) + shift


def _const_spec(shape):
    nd = len(shape)
    return pl.BlockSpec(shape, lambda *_: (0,) * nd)


def _ada_kernel(c_ref, w_ref, b_ref, o_ref):
    cc = c_ref[...]
    o_ref[0, 0] = _dot_split_both(cc * _sigmoid(cc), w_ref[0]) + b_ref[0, 0]


def _ada_mods(cc, ada_w, ada_b):
    depth, d, _ = ada_w.shape
    out = pl.pallas_call(
        _ada_kernel,
        grid=(depth, N_MOD),
        in_specs=[
            pl.BlockSpec((MOD_ROWS, d), lambda i, n: (0, 0)),
            pl.BlockSpec((1, d, d), lambda i, n: (i, 0, n)),
            pl.BlockSpec((1, 1, 1, d), lambda i, n: (i, n, 0, 0)),
        ],
        out_specs=pl.BlockSpec((1, 1, MOD_ROWS, d), lambda i, n: (i, n, 0, 0)),
        out_shape=jax.ShapeDtypeStruct((depth, N_MOD, MOD_ROWS, d), F32),
        compiler_params=_cparams(("arbitrary", "arbitrary")),
    )(cc, ada_w, ada_b.reshape(depth, N_MOD, 1, d))
    return out.transpose(0, 2, 1, 3)


def _rope(x, cos, sin):
    width = x.shape[1]
    reps = width // LANES
    cc = jnp.concatenate([cos] * reps, axis=1)
    ss = jnp.concatenate([sin] * reps, axis=1)
    lane = lax.broadcasted_iota(jnp.int32, x.shape, 1)
    partner = jnp.where((lane & 1) == 0, pltpu.roll(x, width - 1, 1), pltpu.roll(x, 1, 1))
    return x * cc + partner * ss


def _token_tile_specs(tm, d, n_ctx_tiles):
    return [pl.BlockSpec((1, tm, d), lambda bi, ti: (bi, jnp.minimum(ti, n_ctx_tiles - 1), 0)),
            pl.BlockSpec((1, tm, d), lambda bi, ti: (bi, jnp.maximum(ti - n_ctx_tiles, 0), 0))]


def _token_tile(c_ref, x_ref, n_ctx_tiles):
    return jnp.where(pl.program_id(1) < n_ctx_tiles, c_ref[0], x_ref[0])


def _attn_pre_kernel(c_ref, x_ref, mod_ref, w_ref, qg_ref, kg_ref, hmean_ref, cos_ref, sin_ref,
                     q_ref, kt_ref, v_ref, *, q_dim, kv_dim, n_ctx_tiles):
    mod = mod_ref[0]
    h = _modulate(_token_tile(c_ref, x_ref, n_ctx_tiles), mod[0:1], mod[1:2])
    qkv = _dot(h, w_ref[...])
    q = qkv[:, :q_dim]
    k = qkv[:, q_dim:q_dim + kv_dim]
    v = qkv[:, q_dim + kv_dim:]
    hmean = hmean_ref[...]
    q = q * lax.rsqrt(_head_sum(q * q, hmean) + NORM_EPS) * qg_ref[...]
    k = k * lax.rsqrt(_head_sum(k * k, hmean) + NORM_EPS) * kg_ref[...]
    cos = cos_ref[...]
    sin = sin_ref[...]
    q = _rope(q, cos, sin) * (HEAD_DIM ** -0.5 * math.log2(math.e))
    k = _rope(k, cos, sin)
    q_ref[0] = q.astype(BF16)
    kt_ref[0] = k.T.astype(BF16)
    rows = v.shape[0]
    one_col = (lax.broadcasted_iota(jnp.int32, (rows, HEAD_DIM), 1) == 0).astype(F32)
    for g in range(N_KV_HEADS):
        v_ref[0, g] = jnp.concatenate([v[:, g * HEAD_DIM:(g + 1) * HEAD_DIM], one_col], axis=1).astype(BF16)


def _attn_pre(ctx, x, mods, w_qkv, q_gain, k_gain, cos_t, sin_t, n_ctx_tiles):
    b, seq, d = x.shape
    t = ctx.shape[1] + seq
    q_dim = N_HEADS * HEAD_DIM
    kv_dim = N_KV_HEADS * HEAD_DIM
    tm = TOKEN_TILE
    nb = b
    mod_map = lambda bi, ti: (jnp.where(ti < n_ctx_tiles, nb, bi), 0, 0)
    return pl.pallas_call(
        functools.partial(_attn_pre_kernel, q_dim=q_dim, kv_dim=kv_dim, n_ctx_tiles=n_ctx_tiles),
        grid=(b, t // tm),
        in_specs=_token_tile_specs(tm, d, n_ctx_tiles) + [
            pl.BlockSpec((1, N_MOD, d), mod_map),
            _const_spec((d, q_dim + 2 * kv_dim)),
            _const_spec((1, q_dim)),
            _const_spec((1, kv_dim)),
            _const_spec((MXU_WIDTH, MXU_WIDTH)),
            pl.BlockSpec((tm, LANES), lambda bi, ti: (ti, 0)),
            pl.BlockSpec((tm, LANES), lambda bi, ti: (ti, 0)),
        ],
        out_specs=[
            pl.BlockSpec((1, tm, q_dim), lambda bi, ti: (bi, ti, 0)),
            pl.BlockSpec((1, kv_dim, tm), lambda bi, ti: (bi, 0, ti)),
            pl.BlockSpec((1, N_KV_HEADS, tm, LANES), lambda bi, ti: (bi, 0, ti, 0)),
        ],
        out_shape=[
            jax.ShapeDtypeStruct((b, t, q_dim), BF16),
            jax.ShapeDtypeStruct((b, kv_dim, t), BF16),
            jax.ShapeDtypeStruct((b, N_KV_HEADS, t, LANES), BF16),
        ],
        compiler_params=_cparams(("parallel", "parallel")),
    )(ctx, x, mods, w_qkv.astype(BF16), jnp.tile(q_gain, N_HEADS)[None], jnp.tile(k_gain, N_KV_HEADS)[None],
      _head_indicator(HEAD_DIM, 1.0 / HEAD_DIM), cos_t, sin_t)


def _flash_kernel(q_ref, kt_ref, v_ref, o_ref, q4_ref, *, n_ctx_q, ctx_len, kv_tiles):
    qi = pl.program_id(2)
    tq = q_ref.shape[1]
    n_heads = q_ref.shape[2] // HEAD_DIM
    q = q_ref[0]
    for h in range(n_heads):
        q4_ref[h * tq:(h + 1) * tq] = q[:, h * HEAD_DIM:(h + 1) * HEAD_DIM]

    def attend(tiles):
        steps = [(start, size, g * GROUP + j) for g in range(n_heads // GROUP) for start, size in tiles
                 for j in range(GROUP)]

        def scores(st):
            start, size, h = st
            g = h // GROUP
            return jnp.dot(q4_ref[h * tq:(h + 1) * tq], kt_ref[0, g * HEAD_DIM:(g + 1) * HEAD_DIM, start:start + size],
                           preferred_element_type=F32)

        m = [None] * n_heads
        acc = [None] * n_heads
        s_next = scores(steps[0])
        for i, (start, size, h) in enumerate(steps):
            s = s_next
            if i + 1 < len(steps):
                s_next = scores(steps[i + 1])
            cols = [s[:, t * LANES:(t + 1) * LANES] for t in range(size // LANES)]
            m_cur = jnp.max(functools.reduce(jnp.maximum, cols), axis=1, keepdims=True)
            m_new = jnp.broadcast_to(m_cur, (tq, LANES)) if m[h] is None else jnp.maximum(m[h], m_cur)
            p = jnp.concatenate([jnp.exp2(c - m_new) for c in cols], axis=1).astype(BF16)
            pv = jnp.dot(p, v_ref[0, h // GROUP, start:start + size, :], preferred_element_type=F32)
            acc[h] = pv if acc[h] is None else jnp.exp2(m[h] - m_new) * acc[h] + pv
            m[h] = m_new
        o = [a[:, :HEAD_DIM] / a[:, HEAD_DIM:HEAD_DIM + 1] for a in acc]
        o_ref[0] = jnp.concatenate(o, axis=1).astype(BF16)

    @pl.when(qi < n_ctx_q)
    def _():
        attend([(0, ctx_len)])

    @pl.when(qi >= n_ctx_q)
    def _():
        attend(kv_tiles)


def _flash(q, kt, v, ctx_len):
    b, t, q_dim = q.shape
    tq = Q_TILE
    kvh = KV_HEADS_PER_STEP
    gw = kvh * GROUP * HEAD_DIM
    n_kv = next(n for n in range(1, t // LANES + 1) if t % (n * LANES) == 0 and t // n <= KV_TILE)
    kv_tiles = [(i * (t // n_kv), t // n_kv) for i in range(n_kv)]
    kern = functools.partial(_flash_kernel, n_ctx_q=ctx_len // tq, ctx_len=ctx_len, kv_tiles=kv_tiles)
    return pl.pallas_call(
        kern,
        grid=(b, N_KV_HEADS // kvh, t // tq),
        in_specs=[
            pl.BlockSpec((1, tq, gw), lambda bi, g, qi: (bi, qi, g)),
            pl.BlockSpec((1, kvh * HEAD_DIM, t), lambda bi, g, qi: (bi, g, 0)),
            pl.BlockSpec((1, kvh, t, LANES), lambda bi, g, qi: (bi, g, 0, 0)),
        ],
        out_specs=pl.BlockSpec((1, tq, gw), lambda bi, g, qi: (bi, qi, g)),
        out_shape=jax.ShapeDtypeStruct((b, t, q_dim), BF16),
        scratch_shapes=[pltpu.VMEM((kvh * GROUP * tq, HEAD_DIM), BF16)],
        compiler_params=_cparams(("parallel", "parallel", "arbitrary")),
    )(q, kt, v)


def _attn_out_kernel(c_ref, x_ref, o_ref, mod_ref, w_ref, out_ref, *, n_ctx_tiles):
    y = jnp.dot(o_ref[0], w_ref[...], preferred_element_type=F32)
    out_ref[0] = _token_tile(c_ref, x_ref, n_ctx_tiles) + mod_ref[0][2:3] * y


def _attn_out(ctx, x, o, mods, w_o, n_ctx_tiles):
    b, seq, d = x.shape
    t = ctx.shape[1] + seq
    tm = TOKEN_TILE
    nb = b
    mod_map = lambda bi, ti: (jnp.where(ti < n_ctx_tiles, nb, bi), 0, 0)
    return pl.pallas_call(
        functools.partial(_attn_out_kernel, n_ctx_tiles=n_ctx_tiles),
        grid=(b, t // tm),
        in_specs=_token_tile_specs(tm, d, n_ctx_tiles) + [
            pl.BlockSpec((1, tm, o.shape[2]), lambda bi, ti: (bi, ti, 0)),
            pl.BlockSpec((1, N_MOD, d), mod_map),
            _const_spec(w_o.shape),
        ],
        out_specs=pl.BlockSpec((1, tm, d), lambda bi, ti: (bi, ti, 0)),
        out_shape=jax.ShapeDtypeStruct((b, t, d), F32),
        compiler_params=_cparams(("parallel", "parallel")),
    )(ctx, x, o, mods, w_o.astype(BF16))


def _halo_specs(tm, d, n_row_blocks):
    per = tm // HALO
    return [
        pl.BlockSpec((1, tm, d), lambda bi, ti: (bi, ti, 0)),
        pl.BlockSpec((1, HALO, d), lambda bi, ti: (bi, jnp.maximum(ti * per - 1, 0), 0)),
        pl.BlockSpec((1, HALO, d), lambda bi, ti: (bi, jnp.minimum((ti + 1) * per, n_row_blocks - 1), 0)),
    ]


def _edge_keep(ti, n_ctx_tiles, n_tiles):
    first = (ti == 0) | (ti == n_ctx_tiles)
    last = (ti == n_tiles - 1) | (ti == n_ctx_tiles - 1)
    return jnp.where(first, 0.0, 1.0).astype(F32), jnp.where(last, 0.0, 1.0).astype(F32)


def _ffn_tail(xp, xm, xn, mod, keep_prev, keep_next, wup_ref, cw_ref, cb_ref, wdn_ref, n_fc):
    tm = xm.shape[0]
    ext = tm + 2 * HALO
    shift, scale, gate = mod[3:4], mod[4:5], mod[5:6]
    h = jnp.concatenate([_modulate(xp, shift, scale) * keep_prev,
                         _modulate(xm, shift, scale),
                         _modulate(xn, shift, scale) * keep_next], axis=0).astype(BF16)

    def conv(u, cw, cb):
        y = cw[0:1] * pltpu.roll(u, 1, 0) + cw[1:2] * u + cw[2:3] * pltpu.roll(u, ext - 1, 0) + cb
        return y[HALO:HALO + tm]

    col = lambda j: slice(j * FF_CHUNK, (j + 1) * FF_CHUNK)
    up = lambda j: (jnp.dot(h, wup_ref[:, col(j)], preferred_element_type=F32),
                    jnp.dot(h, wup_ref[:, col(n_fc + j)], preferred_element_type=F32))
    acc = None
    ahead = [up(0), up(1)]
    for j in range(n_fc):
        ug, uv = ahead.pop(0)
        if j + 2 < n_fc:
            ahead.append(up(j + 2))
        cg = conv(ug, cw_ref[:, col(j)], cb_ref[:, col(j)])
        cv = conv(uv, cw_ref[:, col(n_fc + j)], cb_ref[:, col(n_fc + j)])
        act = cg * _sigmoid(cg) * cv
        down = jnp.dot(act.astype(BF16), wdn_ref[col(j), :], preferred_element_type=F32)
        acc = down if acc is None else acc + down
    return xm + gate * acc


def _ffn_kernel(xm_ref, xp_ref, xn_ref, mod_ref, wup_ref, cw_ref, cb_ref, wdn_ref, out_ref,
                *, n_ctx_tiles, n_tiles, n_fc):
    keep_prev, keep_next = _edge_keep(pl.program_id(1), n_ctx_tiles, n_tiles)
    out_ref[0] = _ffn_tail(xp_ref[0], xm_ref[0], xn_ref[0], mod_ref[0], keep_prev, keep_next,
                           wup_ref, cw_ref, cb_ref, wdn_ref, n_fc)


def _rwkv_out_ffn_kernel(x_ref, xp_ref, xn_ref, yf_ref, yfp_ref, yfn_ref, yb_ref, ybp_ref, ybn_ref,
                         bv_ref, bvp_ref, bvn_ref, g_ref, gp_ref, gn_ref, mod_ref, gnw_ref, gnb_ref, gmean_ref,
                         wo_ref, wup_ref, cw_ref, cb_ref, wdn_ref, fg_ref, out_ref, *, n_tiles, n_fc):
    tm = x_ref.shape[1]
    rows = lambda p, m, n: jnp.concatenate([p[0], m[0], n[0]], axis=0)
    mod = mod_ref[0]
    y = rows(yfp_ref, yf_ref, yfn_ref).astype(F32) + rows(ybp_ref, yb_ref, ybn_ref).astype(F32)
    gmean = gmean_ref[...]
    dev = y - _head_sum(y, gmean)
    var = _head_sum(dev * dev, gmean)
    y_n = dev * lax.rsqrt(var + GN_EPS) * gnw_ref[...] + gnb_ref[...]
    z = (y_n + rows(bvp_ref, bv_ref, bvn_ref)) * rows(gp_ref, g_ref, gn_ref)
    x1 = rows(xp_ref, x_ref, xn_ref) + mod[2:3] * _dot(z, wo_ref[...])
    keep_prev, keep_next = _edge_keep(pl.program_id(1), 0, n_tiles)
    lo = WIDE_HALO
    out = _ffn_tail(x1[lo - HALO:lo], x1[lo:lo + tm], x1[lo + tm:lo + tm + HALO], mod, keep_prev, keep_next,
                    wup_ref, cw_ref, cb_ref, wdn_ref, n_fc)
    out_ref[0] = _rms_norm(out) * fg_ref[...]


def _conv_ffn(xa, mods, w_up, conv_w, conv_b, w_down, n_ctx_tiles, tm):
    b, t, d = xa.shape
    n_fc = w_down.shape[0] // FF_CHUNK
    n_tiles = t // tm
    nb = b
    weights = [w_up.astype(BF16), conv_w, conv_b[None], w_down.astype(BF16)]
    mod_map = lambda bi, ti: (jnp.where(ti < n_ctx_tiles, nb, bi), 0, 0)
    return pl.pallas_call(
        functools.partial(_ffn_kernel, n_ctx_tiles=n_ctx_tiles, n_tiles=n_tiles, n_fc=n_fc),
        grid=(b, n_tiles),
        in_specs=_halo_specs(tm, d, t // HALO) + [pl.BlockSpec((1, N_MOD, d), mod_map)]
        + [_const_spec(w.shape) for w in weights],
        out_specs=pl.BlockSpec((1, tm, d), lambda bi, ti: (bi, ti, 0)),
        out_shape=jax.ShapeDtypeStruct((b, t, d), F32),
        compiler_params=_cparams(("parallel", "parallel")),
    )(xa, xa, xa, mods, *weights)


def _rwkv_out_conv_ffn(xa, y, bv, g, mods, gn_w, gn_b, w_o, w_up, conv_w, conv_b, w_down, final_gain, n_ctx_tiles):
    b, t, d = xa.shape
    tm = TOKEN_TILE
    n_tiles = t // tm - n_ctx_tiles
    n_fc = w_down.shape[0] // FF_CHUNK
    per = tm // WIDE_HALO
    n_halo_blocks = t // WIDE_HALO
    token_specs = [
        pl.BlockSpec((1, tm, d), lambda bi, ti: (bi, ti + n_ctx_tiles, 0)),
        pl.BlockSpec((1, WIDE_HALO, d), lambda bi, ti: (bi, jnp.maximum((ti + n_ctx_tiles) * per - 1, 0), 0)),
        pl.BlockSpec((1, WIDE_HALO, d),
                     lambda bi, ti: (bi, jnp.minimum((ti + n_ctx_tiles + 1) * per, n_halo_blocks - 1), 0)),
    ]
    consts = [gn_w[None], gn_b[None], _head_indicator(RWKV_HEAD, 1.0 / RWKV_HEAD), w_o.astype(BF16),
              w_up.astype(BF16), conv_w, conv_b[None], w_down.astype(BF16), final_gain[None]]
    tokens = [xa, y[0], y[1], bv, g]
    return pl.pallas_call(
        functools.partial(_rwkv_out_ffn_kernel, n_tiles=n_tiles, n_fc=n_fc),
        grid=(b, n_tiles),
        in_specs=token_specs * len(tokens) + [pl.BlockSpec((1, N_MOD, d), lambda bi, ti: (bi, 0, 0))]
        + [_const_spec(c.shape) for c in consts],
        out_specs=pl.BlockSpec((1, tm, d), lambda bi, ti: (bi, ti, 0)),
        out_shape=jax.ShapeDtypeStruct((b, n_tiles * tm, d), F32),
        compiler_params=_cparams(("parallel", "parallel")),
    )(*[a for a in tokens for _ in range(3)], mods, *consts)


def _rwkv_pre_kernel(xm_ref, xp_ref, xn_ref, mod_ref, mu_ref, wr_ref, wk_ref, wv_ref,
                     dw1_ref, dw2_ref, w0_ref, a1_ref, a2_ref, a0_ref, g1_ref, g2_ref,
                     kk_ref, ka_ref, rk_ref, gsum_ref,
                     r_out, v_out, k_out, kk_out, a_out, ld_out, g_out, bv_out,
                     *, n_ctx_tiles, n_tiles):
    tm = xm_ref.shape[1]
    d = xm_ref.shape[2]
    ext = tm + 2 * HALO
    mod = mod_ref[0]
    shift, scale = mod[0:1], mod[1:2]
    keep_prev, keep_next = _edge_keep(pl.program_id(1), n_ctx_tiles, n_tiles)
    h = _modulate(xm_ref[0], shift, scale)
    he = jnp.concatenate([_modulate(xp_ref[0], shift, scale) * keep_prev, h,
                          _modulate(xn_ref[0], shift, scale) * keep_next], axis=0)
    xx = 0.5 * (pltpu.roll(he, 1, 0) + pltpu.roll(he, ext - 1, 0))[HALO:HALO + tm] - h
    mu = mu_ref[...]
    mix = lambda i: h + xx * mu[i:i + 1]
    r = _dot(mix(0), wr_ref[...])
    k = _dot(mix(2), wk_ref[...])
    v = _dot(mix(3), wv_ref[...])
    z = w0_ref[...] + _dot(jnp.tanh(_dot(mix(1), dw1_ref[...])), dw2_ref[...])
    log_decay = -math.exp(-0.5) * _sigmoid(z)
    a = _sigmoid(a0_ref[...] + _dot(_dot(mix(4), a1_ref[...]), a2_ref[...]))
    g = _dot(_sigmoid(_dot(mix(5), g1_ref[...])), g2_ref[...])
    gsum = gsum_ref[...]
    kx = k * kk_ref[...]
    kk = kx / jnp.maximum(jnp.sqrt(_head_sum(kx * kx, gsum)), 1e-12)
    ka = ka_ref[...]
    bonus = (r * rk_ref[...]) * k * (2.0 + (a[:, :d] + a[:, d:] - 2.0) * ka)
    r_out[0] = r
    v_out[0] = v.astype(BF16)
    k_out[0] = k
    kk_out[0] = kk
    a_out[0, 0] = a[:, :d]
    a_out[1, 0] = a[:, d:]
    ld_out[0, 0] = log_decay[:, :d]
    ld_out[1, 0] = log_decay[:, d:]
    g_out[0] = g.astype(BF16)
    bv_out[0] = (_head_sum(bonus, gsum) * v).astype(BF16)


def _block_diag2(m0, m1):
    z = jnp.zeros_like(m0)
    return jnp.concatenate([jnp.concatenate([m0, z], axis=1), jnp.concatenate([z, m1], axis=1)], axis=0)


def _rwkv_pre(xa, mods, p, n_ctx_tiles):
    b, t, d = xa.shape
    tm = RWKV_TILE
    n_tiles = t // tm
    nb = b
    n_ctx = n_ctx_tiles * (TOKEN_TILE // tm)
    mod_map = lambda bi, ti: (jnp.where(ti < n_ctx, nb, bi), 0, 0)
    gate_pad = -p['gate_g1'].shape[1] % LANES
    consts = [
        p['mu'],
        p['w_r'].astype(BF16), p['w_k'].astype(BF16), p['w_v'].astype(BF16),
        jnp.concatenate([p['decay_w1'][0], p['decay_w1'][1]], axis=1).astype(BF16),
        _block_diag2(p['decay_w2'][0], p['decay_w2'][1]).astype(BF16),
        p['decay_w0'].reshape(1, 2 * d),
        jnp.concatenate([p['aaa_a1'][0], p['aaa_a1'][1]], axis=1).astype(BF16),
        _block_diag2(p['aaa_a2'][0], p['aaa_a2'][1]).astype(BF16),
        p['aaa_a0'].reshape(1, 2 * d),
        jnp.pad(p['gate_g1'], ((0, 0), (0, gate_pad))).astype(BF16),
        jnp.pad(p['gate_g2'], ((0, gate_pad), (0, 0))).astype(BF16),
        p['k_k'][None], p['k_a'][None], p['r_k'].reshape(1, d),
        _head_indicator(RWKV_HEAD, 1.0),
    ]
    tok = pl.BlockSpec((1, tm, d), lambda bi, ti: (bi, ti, 0))
    tok2 = pl.BlockSpec((2, 1, tm, d), lambda bi, ti: (0, bi, ti, 0))
    s1 = jax.ShapeDtypeStruct((b, t, d), F32)
    s2 = jax.ShapeDtypeStruct((2, b, t, d), F32)
    h1 = jax.ShapeDtypeStruct((b, t, d), BF16)
    return pl.pallas_call(
        functools.partial(_rwkv_pre_kernel, n_ctx_tiles=n_ctx, n_tiles=n_tiles),
        grid=(b, n_tiles),
        in_specs=_halo_specs(tm, d, t // HALO) + [pl.BlockSpec((1, N_MOD, d), mod_map)]
        + [_const_spec(c.shape) for c in consts],
        out_specs=[tok, tok, tok, tok, tok2, tok2, tok, tok],
        out_shape=[s1, h1, s1, s1, s2, s2, h1, h1],
        compiler_params=_cparams(("parallel", "parallel")),
    )(xa, xa, xa, mods, *consts)


def _head_stack(x, lane_head):
    return jnp.concatenate([jnp.where(lane_head == hh, x, 0.0) for hh in range(HEADS_PER_GROUP)], axis=0)


def _wkv_kernel(rf_ref, vf_ref, kf_ref, kkf_ref, af_ref, ldf_ref, rb_ref, vb_ref, kb_ref, kkb_ref, ab_ref, ldb_ref,
                ka_ref, yf_ref, yb_ref, state_ref):
    gw = HEADS_PER_GROUP * RWKV_HEAD
    n_groups = rf_ref.shape[2] // gw

    @pl.when(pl.program_id(1) == 0)
    def _():
        state_ref[...] = jnp.zeros(state_ref.shape, F32)

    row = lax.broadcasted_iota(jnp.int32, (CHUNK, gw), 0)
    lane = lax.broadcasted_iota(jnp.int32, (CHUNK, gw), 1)
    col = lane % CHUNK
    lane_head = lane // RWKV_HEAD
    eye = (row == col).astype(F32)
    bd_r = lax.broadcasted_iota(jnp.int32, (gw, gw), 0) // RWKV_HEAD
    bd_c = lax.broadcasted_iota(jnp.int32, (gw, gw), 1) // RWKV_HEAD
    block_diag = bd_r == bd_c
    ka = ka_ref[...]
    stack = lambda x: _head_stack(x, lane_head)
    cols = [slice(gi * gw, (gi + 1) * gw) for gi in range(n_groups)]

    n_sub = rf_ref.shape[1] // CHUNK
    wide = {}
    for di, (refs, incl, last) in enumerate((((rf_ref, vf_ref, kf_ref, kkf_ref, af_ref, ldf_ref), row >= col, CHUNK - 1),
                                             ((rb_ref, vb_ref, kb_ref, kkb_ref, ab_ref, ldb_ref), row <= col, 0))):
        r_ref, v_ref, k_ref, kk_ref, a_ref, ld_ref = refs
        tri = jnp.where(incl[:, :CHUNK], 1.0, 0.0).astype(BF16)
        for wi in range(n_sub):
            ci = wi if di == 0 else n_sub - 1 - wi
            rows = slice(ci * CHUNK, (ci + 1) * CHUNK)
            ld = ld_ref[0, 0, rows]
            ag = a_ref[0, 0, rows]
            kk = kk_ref[0, rows]
            kc = k_ref[0, rows] * (1.0 + (ag - 1.0) * ka)
            bc = kk * ag
            log_p = _dot_split_lhs_rhs(tri, ld)
            log_p_last = log_p[last:last + 1]
            dec_out = jnp.exp(-log_p)
            dec_tail = jnp.exp(log_p_last - log_p)
            wide[di, wi] = dict(
                rows=rows, incl=incl, strict=incl & (row != col), vc=v_ref[0, rows].astype(F32),
                ar=jnp.concatenate([-kk * jnp.exp(log_p - ld), r_ref[0, rows] * jnp.exp(log_p)], axis=0),
                b_out=bc * dec_out, k_out=kc * dec_out,
                bk=jnp.concatenate([bc * dec_tail, kc * dec_tail], axis=0),
                state_decay=jnp.exp(log_p_last))

    chains = [(di, wi, gi) for wi in range(n_sub) for di in range(2) for gi in range(n_groups)]
    w = lambda ch, name: wide[ch[0], ch[1]][name]
    part = lambda ch, name: wide[ch[0], ch[1]][name][:, cols[ch[2]]]
    xb = [_dot_nt(part(ch, 'ar'), stack(part(ch, 'b_out'))) for ch in chains]
    xk = [_dot_nt(part(ch, 'ar'), stack(part(ch, 'k_out'))) for ch in chains]
    a_ab = [jnp.where(w(ch, 'strict'), x[:CHUNK], 0.0) for ch, x in zip(chains, xb)]
    a_rb = [jnp.where(w(ch, 'incl'), x[CHUNK:], 0.0) for ch, x in zip(chains, xb)]
    a_kk = [jnp.concatenate([jnp.where(w(ch, 'strict'), x[:CHUNK], 0.0), jnp.where(w(ch, 'incl'), x[CHUNK:], 0.0)],
                            axis=0) for ch, x in zip(chains, xk)]
    av = [_dot(a, stack(part(ch, 'vc'))) for ch, a in zip(chains, a_kk)]

    m = [eye + jnp.where(row // 2 == col // 2, a, 0.0) for a in a_ab]
    s = 2
    while s < CHUNK:
        couples = (row // (2 * s) == col // (2 * s)) & (row // s != col // s)
        t_off = [_dot(jnp.where(couples, a, 0.0), stack(mm)) for a, mm in zip(a_ab, m)]
        m = [mm + _dot(mm, stack(t)) for mm, t in zip(m, t_off)]
        s *= 2

    state = {(di, gi): state_ref[di * n_groups + gi] for di in range(2) for gi in range(n_groups)}
    for wi in range(n_sub):
        idx = [i for i, ch in enumerate(chains) if ch[1] == wi]
        arh = {i: _dot_nt(part(chains[i], 'ar'), state[chains[i][0], chains[i][2]]) for i in idx}
        u = {i: _dot(m[i], stack(arh[i][:CHUNK] + av[i][:CHUNK])) for i in idx}
        y = {i: arh[i][CHUNK:] + _dot(a_rb[i], stack(u[i])) + av[i][CHUNK:] for i in idx}
        upd = {i: _dot_tn(jnp.concatenate([u[i], part(chains[i], 'vc')], axis=0), part(chains[i], 'bk')) for i in idx}
        for di, y_ref in ((0, yf_ref), (1, yb_ref)):
            y_ref[0, wide[di, wi]['rows']] = jnp.concatenate([y[i] for i in idx if chains[i][0] == di],
                                                             axis=1).astype(y_ref.dtype)
        for i in idx:
            di, _, gi = chains[i]
            state[di, gi] = part(chains[i], 'state_decay') * state[di, gi] + jnp.where(block_diag, upd[i], 0.0)
    for (di, gi), s in state.items():
        state_ref[di * n_groups + gi] = s


def _dot_split_lhs_rhs(tri_bf16, x):
    hi, lo = _split(x)
    return (jnp.dot(tri_bf16, hi, preferred_element_type=F32)
            + jnp.dot(tri_bf16, lo, preferred_element_type=F32))


def _wkv(r, v, k, kk, a, ld, k_a, ctx_len):
    b, t, d = r.shape
    rows = WKV_STEP_CHUNKS * CHUNK
    assert ctx_len % rows == 0 and t % rows == 0
    n_steps = t // rows
    n_ctx = ctx_len // rows

    def back(ci):
        return jnp.where(ci < n_ctx, n_ctx - 1 - ci, n_steps - 1 - (ci - n_ctx))

    fwd = pl.BlockSpec((1, rows, d), lambda bi, ci: (bi, ci, 0))
    bwd = pl.BlockSpec((1, rows, d), lambda bi, ci: (bi, back(ci), 0))
    fwd2 = pl.BlockSpec((1, 1, rows, d), lambda bi, ci: (0, bi, ci, 0))
    bwd2 = pl.BlockSpec((1, 1, rows, d), lambda bi, ci: (1, bi, back(ci), 0))
    gw = HEADS_PER_GROUP * RWKV_HEAD
    y_shape = jax.ShapeDtypeStruct((b, t, d), BF16)
    return pl.pallas_call(
        _wkv_kernel,
        grid=(b, n_steps),
        in_specs=[fwd, fwd, fwd, fwd, fwd2, fwd2, bwd, bwd, bwd, bwd, bwd2, bwd2, _const_spec((1, d))],
        out_specs=[fwd, bwd],
        out_shape=[y_shape, y_shape],
        scratch_shapes=[pltpu.VMEM((2 * d // gw, gw, gw), F32)],
        compiler_params=_cparams(("arbitrary", "arbitrary")),
    )(r, v, k, kk, a, ld, r, v, k, kk, a, ld, k_a[None])


def _rope_tables(seq, ctx_len):
    rows = seq // GRID_W
    row = jnp.repeat(jnp.arange(rows, dtype=F32), GRID_W)
    col = jnp.tile(jnp.arange(GRID_W, dtype=F32), rows)
    n_freq = HEAD_DIM // 4
    inv_freq = ROPE_THETA ** (-jnp.arange(n_freq, dtype=F32) / n_freq)
    ang = jnp.concatenate([row[:, None] * inv_freq, col[:, None] * inv_freq], axis=-1)
    cos = jnp.repeat(jnp.cos(ang), 2, axis=1)
    sin = jnp.repeat(jnp.sin(ang), 2, axis=1) * jnp.tile(jnp.array([-1.0, 1.0], F32), HEAD_DIM // 2)
    cos = jnp.concatenate([jnp.ones((ctx_len, HEAD_DIM), F32), cos], axis=0)
    sin = jnp.concatenate([jnp.zeros((ctx_len, HEAD_DIM), F32), sin], axis=0)
    reps = LANES // HEAD_DIM
    return jnp.tile(cos, (1, reps)), jnp.tile(sin, (1, reps))


def kernel(x, c, ctx, c_ctx, ada_w, ada_b, attn_w_qkv, attn_q_gain, attn_k_gain, attn_w_o, rwkv_mu, rwkv_w_r, rwkv_w_k, rwkv_w_v, rwkv_w_o, rwkv_decay_w0, rwkv_decay_w1, rwkv_decay_w2, rwkv_aaa_a0, rwkv_aaa_a1, rwkv_aaa_a2, rwkv_gate_g1, rwkv_gate_g2, rwkv_k_k, rwkv_k_a, rwkv_r_k, rwkv_gn_w, rwkv_gn_b, ffn_w_up, ffn_conv_w, ffn_conv_b, ffn_w_down, final_gain):
    b, seq, d = x.shape
    ctx_len = ctx.shape[1]
    assert ada_w.shape[0] == 2 and attn_w_qkv.shape[0] == 1 and rwkv_w_r.shape[0] == 1
    assert b + 1 <= MOD_ROWS and ctx_len % TOKEN_TILE == 0 and seq % TOKEN_TILE == 0 and seq % GRID_W == 0
    n_ctx_tiles = ctx_len // TOKEN_TILE

    cc = jnp.zeros((MOD_ROWS, d), F32).at[:b].set(c).at[b].set(c_ctx)
    mods = _ada_mods(cc, ada_w, ada_b)

    cos_t, sin_t = _rope_tables(seq, ctx_len)
    q, kt, v = _attn_pre(ctx, x, mods[0], attn_w_qkv[0], attn_q_gain[0], attn_k_gain[0], cos_t, sin_t, n_ctx_tiles)
    o = _flash(q, kt, v, ctx_len)
    xa = _attn_out(ctx, x, o, mods[0], attn_w_o[0], n_ctx_tiles)
    xa = _conv_ffn(xa, mods[0], ffn_w_up[0], ffn_conv_w[0], ffn_conv_b[0], ffn_w_down[0], n_ctx_tiles, TOKEN_TILE)

    p = dict(mu=rwkv_mu[0], w_r=rwkv_w_r[0], w_k=rwkv_w_k[0], w_v=rwkv_w_v[0],
             decay_w0=rwkv_decay_w0[0], decay_w1=rwkv_decay_w1[0], decay_w2=rwkv_decay_w2[0],
             aaa_a0=rwkv_aaa_a0[0], aaa_a1=rwkv_aaa_a1[0], aaa_a2=rwkv_aaa_a2[0],
             gate_g1=rwkv_gate_g1[0], gate_g2=rwkv_gate_g2[0],
             k_k=rwkv_k_k[0], k_a=rwkv_k_a[0], r_k=rwkv_r_k[0])
    r, vv, k, kk, a, ld, g, bv = _rwkv_pre(xa, mods[1], p, n_ctx_tiles)
    y = _wkv(r, vv, k, kk, a, ld, rwkv_k_a[0], ctx_len)
    return _rwkv_out_conv_ffn(xa, y, bv, g, mods[1], rwkv_gn_w[0], rwkv_gn_b[0], rwkv_w_o[0],
                              ffn_w_up[1], ffn_conv_w[1], ffn_conv_b[1], ffn_w_down[1], final_gain, n_ctx_tiles)
```

```python
import functools
import math

import jax
import jax.numpy as jnp
from jax import lax
from jax.experimental import pallas as pl
from jax.experimental.pallas import tpu as pltpu

F32 = jnp.float32
BF16 = jnp.bfloat16

LANES = 128
SUBLANES = 8
MXU_WIDTH = 256
VMEM_LIMIT_BYTES = 56 * 1024 * 1024

N_HEADS = 16
N_KV_HEADS = 4
HEAD_DIM = 64
GROUP = N_HEADS // N_KV_HEADS
RWKV_HEAD = 64
GRID_W = 64
ROPE_THETA = 10000.0
NORM_EPS = 1e-6
GN_EPS = 64e-5
N_MOD = 6
MOD_ROWS = 8

TOKEN_TILE = 256
RWKV_TILE = 256
Q_TILE = 256
KV_TILE = 2816
KV_HEADS_PER_STEP = 1
FF_CHUNK = 256
CHUNK = 64
HEADS_PER_GROUP = 4
WKV_STEP_CHUNKS = 4
HALO = SUBLANES
WIDE_HALO = 2 * SUBLANES


def _cparams(sem):
    return pltpu.CompilerParams(dimension_semantics=sem, vmem_limit_bytes=VMEM_LIMIT_BYTES)


def _dot(a, b):
    return jnp.dot(a.astype(BF16), b.astype(BF16), preferred_element_type=F32)


def _dot_nt(a, b):
    return lax.dot_general(a.astype(BF16), b.astype(BF16), (((1,), (1,)), ((), ())),
                           preferred_element_type=F32)


def _dot_tn(a, b):
    return lax.dot_general(a.astype(BF16), b.astype(BF16), (((0,), (0,)), ((), ())),
                           preferred_element_type=F32)


def _split(a):
    hi = a.astype(BF16)
    lo = (a - hi.astype(F32)).astype(BF16)
    return hi, lo


def _head_indicator(head_dim, scale):
    return jnp.kron(jnp.eye(MXU_WIDTH // head_dim, dtype=F32), jnp.full((head_dim, head_dim), scale, F32)).astype(BF16)


def _head_sum(a, indicator):
    a = a.astype(BF16)
    blocks = [jnp.dot(a[:, j:j + MXU_WIDTH], indicator, preferred_element_type=F32)
              for j in range(0, a.shape[1], MXU_WIDTH)]
    return blocks[0] if len(blocks) == 1 else jnp.concatenate(blocks, axis=1)


def _dot_split_both(a, b):
    ah, al = _split(a)
    bh, bl = _split(b)
    d = functools.partial(jnp.dot, preferred_element_type=F32)
    return d(ah, bh) + (d(ah, bl) + d(al, bh))


def _sigmoid(x):
    return 1.0 / (1.0 + jnp.exp(-x))


def _rms_norm(x):
    return x * lax.rsqrt(jnp.mean(x * x, axis=-1, keepdims=True) + NORM_EPS)


def _modulate(x, shift, scale):
    return _rms_norm(x) * (1.0 + scale) + shift


def _const_spec(shape):
    nd = len(shape)
    return pl.BlockSpec(shape, lambda *_: (0,) * nd)


def _ada_kernel(c_ref, w_ref, b_ref, o_ref):
    cc = c_ref[...]
    o_ref[0, 0] = _dot_split_both(cc * _sigmoid(cc), w_ref[0]) + b_ref[0, 0]


def _ada_mods(cc, ada_w, ada_b):
    depth, d, _ = ada_w.shape
    out = pl.pallas_call(
        _ada_kernel,
        grid=(depth, N_MOD),
        in_specs=[
            pl.BlockSpec((MOD_ROWS, d), lambda i, n: (0, 0)),
            pl.BlockSpec((1, d, d), lambda i, n: (i, 0, n)),
            pl.BlockSpec((1, 1, 1, d), lambda i, n: (i, n, 0, 0)),
        ],
        out_specs=pl.BlockSpec((1, 1, MOD_ROWS, d), lambda i, n: (i, n, 0, 0)),
        out_shape=jax.ShapeDtypeStruct((depth, N_MOD, MOD_ROWS, d), F32),
        compiler_params=_cparams(("arbitrary", "arbitrary")),
    )(cc, ada_w, ada_b.reshape(depth, N_MOD, 1, d))
    return out.transpose(0, 2, 1, 3)


def _rope(x, cos, sin):
    width = x.shape[1]
    reps = width // LANES
    cc = jnp.concatenate([cos] * reps, axis=1)
    ss = jnp.concatenate([sin] * reps, axis=1)
    lane = lax.broadcasted_iota(jnp.int32, x.shape, 1)
    partner = jnp.where((lane & 1) == 0, pltpu.roll(x, width - 1, 1), pltpu.roll(x, 1, 1))
    return x * cc + partner * ss


def _token_tile_specs(tm, d, n_ctx_tiles):
    return [pl.BlockSpec((1, tm, d), lambda bi, ti: (bi, jnp.minimum(ti, n_ctx_tiles - 1), 0)),
            pl.BlockSpec((1, tm, d), lambda bi, ti: (bi, jnp.maximum(ti - n_ctx_tiles, 0), 0))]


def _token_tile(c_ref, x_ref, n_ctx_tiles):
    return jnp.where(pl.program_id(1) < n_ctx_tiles, c_ref[0], x_ref[0])


def _attn_pre_kernel(c_ref, x_ref, mod_ref, w_ref, qg_ref, kg_ref, hmean_ref, cos_ref, sin_ref,
                     q_ref, kt_ref, v_ref, *, q_dim, kv_dim, n_ctx_tiles):
    mod = mod_ref[0]
    h = _modulate(_token_tile(c_ref, x_ref, n_ctx_tiles), mod[0:1], mod[1:2])
    qkv = _dot(h, w_ref[...])
    q = qkv[:, :q_dim]
    k = qkv[:, q_dim:q_dim + kv_dim]
    v = qkv[:, q_dim + kv_dim:]
    hmean = hmean_ref[...]
    q = q * lax.rsqrt(_head_sum(q * q, hmean) + NORM_EPS) * qg_ref[...]
    k = k * lax.rsqrt(_head_sum(k * k, hmean) + NORM_EPS) * kg_ref[...]
    cos = cos_ref[...]
    sin = sin_ref[...]
    q = _rope(q, cos, sin) * (HEAD_DIM ** -0.5 * math.log2(math.e))
    k = _rope(k, cos, sin)
    q_ref[0] = q.astype(BF16)
    kt_ref[0] = k.T.astype(BF16)
    rows = v.shape[0]
    one_col = (lax.broadcasted_iota(jnp.int32, (rows, HEAD_DIM), 1) == 0).astype(F32)
    for g in range(N_KV_HEADS):
        v_ref[0, g] = jnp.concatenate([v[:, g * HEAD_DIM:(g + 1) * HEAD_DIM], one_col], axis=1).astype(BF16)


def _attn_pre(ctx, x, mods, w_qkv, q_gain, k_gain, cos_t, sin_t, n_ctx_tiles):
    b, seq, d = x.shape
    t = ctx.shape[1] + seq
    q_dim = N_HEADS * HEAD_DIM
    kv_dim = N_KV_HEADS * HEAD_DIM
    tm = TOKEN_TILE
    nb = b
    mod_map = lambda bi, ti: (jnp.where(ti < n_ctx_tiles, nb, bi), 0, 0)
    return pl.pallas_call(
        functools.partial(_attn_pre_kernel, q_dim=q_dim, kv_dim=kv_dim, n_ctx_tiles=n_ctx_tiles),
        grid=(b, t // tm),
        in_specs=_token_tile_specs(tm, d, n_ctx_tiles) + [
            pl.BlockSpec((1, N_MOD, d), mod_map),
            _const_spec((d, q_dim + 2 * kv_dim)),
            _const_spec((1, q_dim)),
            _const_spec((1, kv_dim)),
            _const_spec((MXU_WIDTH, MXU_WIDTH)),
            pl.BlockSpec((tm, LANES), lambda bi, ti: (ti, 0)),
            pl.BlockSpec((tm, LANES), lambda bi, ti: (ti, 0)),
        ],
        out_specs=[
            pl.BlockSpec((1, tm, q_dim), lambda bi, ti: (bi, ti, 0)),
            pl.BlockSpec((1, kv_dim, tm), lambda bi, ti: (bi, 0, ti)),
            pl.BlockSpec((1, N_KV_HEADS, tm, LANES), lambda bi, ti: (bi, 0, ti, 0)),
        ],
        out_shape=[
            jax.ShapeDtypeStruct((b, t, q_dim), BF16),
            jax.ShapeDtypeStruct((b, kv_dim, t), BF16),
            jax.ShapeDtypeStruct((b, N_KV_HEADS, t, LANES), BF16),
        ],
        compiler_params=_cparams(("parallel", "parallel")),
    )(ctx, x, mods, w_qkv.astype(BF16), jnp.tile(q_gain, N_HEADS)[None], jnp.tile(k_gain, N_KV_HEADS)[None],
      _head_indicator(HEAD_DIM, 1.0 / HEAD_DIM), cos_t, sin_t)


def _flash_kernel(q_ref, kt_ref, v_ref, o_ref, q4_ref, *, n_ctx_q, ctx_len, kv_tiles):
    qi = pl.program_id(2)
    tq = q_ref.shape[1]
    n_heads = q_ref.shape[2] // HEAD_DIM
    q = q_ref[0]
    for h in range(n_heads):
        q4_ref[h * tq:(h + 1) * tq] = q[:, h * HEAD_DIM:(h + 1) * HEAD_DIM]

    def attend(tiles):
        steps = [(start, size, g * GROUP + j) for g in range(n_heads // GROUP) for start, size in tiles
                 for j in range(GROUP)]

        def scores(st):
            start, size, h = st
            g = h // GROUP
            return jnp.dot(q4_ref[h * tq:(h + 1) * tq], kt_ref[0, g * HEAD_DIM:(g + 1) * HEAD_DIM, start:start + size],
                           preferred_element_type=F32)

        m = [None] * n_heads
        acc = [None] * n_heads
        s_next = scores(steps[0])
        for i, (start, size, h) in enumerate(steps):
            s = s_next
            if i + 1 < len(steps):
                s_next = scores(steps[i + 1])
            cols = [s[:, t * LANES:(t + 1) * LANES] for t in range(size // LANES)]
            m_cur = jnp.max(functools.reduce(jnp.maximum, cols), axis=1, keepdims=True)
            m_new = jnp.broadcast_to(m_cur, (tq, LANES)) if m[h] is None else jnp.maximum(m[h], m_cur)
            p = jnp.concatenate([jnp.exp2(c - m_new) for c in cols], axis=1).astype(BF16)
            pv = jnp.dot(p, v_ref[0, h // GROUP, start:start + size, :], preferred_element_type=F32)
            acc[h] = pv if acc[h] is None else jnp.exp2(m[h] - m_new) * acc[h] + pv
            m[h] = m_new
        o = [a[:, :HEAD_DIM] / a[:, HEAD_DIM:HEAD_DIM + 1] for a in acc]
        o_ref[0] = jnp.concatenate(o, axis=1).astype(BF16)

    @pl.when(qi < n_ctx_q)
    def _():
        attend([(0, ctx_len)])

    @pl.when(qi >= n_ctx_q)
    def _():
        attend(kv_tiles)


def _flash(q, kt, v, ctx_len):
    b, t, q_dim = q.shape
    tq = Q_TILE
    kvh = KV_HEADS_PER_STEP
    gw = kvh * GROUP * HEAD_DIM
    n_kv = next(n for n in range(1, t // LANES + 1) if t % (n * LANES) == 0 and t // n <= KV_TILE)
    kv_tiles = [(i * (t // n_kv), t // n_kv) for i in range(n_kv)]
    kern = functools.partial(_flash_kernel, n_ctx_q=ctx_len // tq, ctx_len=ctx_len, kv_tiles=kv_tiles)
    return pl.pallas_call(
        kern,
        grid=(b, N_KV_HEADS // kvh, t // tq),
        in_specs=[
            pl.BlockSpec((1, tq, gw), lambda bi, g, qi: (bi, qi, g)),
            pl.BlockSpec((1, kvh * HEAD_DIM, t), lambda bi, g, qi: (bi, g, 0)),
            pl.BlockSpec((1, kvh, t, LANES), lambda bi, g, qi: (bi, g, 0, 0)),
        ],
        out_specs=pl.BlockSpec((1, tq, gw), lambda bi, g, qi: (bi, qi, g)),
        out_shape=jax.ShapeDtypeStruct((b, t, q_dim), BF16),
        scratch_shapes=[pltpu.VMEM((kvh * GROUP * tq, HEAD_DIM), BF16)],
        compiler_params=_cparams(("parallel", "parallel", "arbitrary")),
    )(q, kt, v)


def _attn_out_kernel(c_ref, x_ref, o_ref, mod_ref, w_ref, out_ref, *, n_ctx_tiles):
    y = jnp.dot(o_ref[0], w_ref[...], preferred_element_type=F32)
    out_ref[0] = _token_tile(c_ref, x_ref, n_ctx_tiles) + mod_ref[0][2:3] * y


def _attn_out(ctx, x, o, mods, w_o, n_ctx_tiles):
    b, seq, d = x.shape
    t = ctx.shape[1] + seq
    tm = TOKEN_TILE
    nb = b
    mod_map = lambda bi, ti: (jnp.where(ti < n_ctx_tiles, nb, bi), 0, 0)
    return pl.pallas_call(
        functools.partial(_attn_out_kernel, n_ctx_tiles=n_ctx_tiles),
        grid=(b, t // tm),
        in_specs=_token_tile_specs(tm, d, n_ctx_tiles) + [
            pl.BlockSpec((1, tm, o.shape[2]), lambda bi, ti: (bi, ti, 0)),
            pl.BlockSpec((1, N_MOD, d), mod_map),
            _const_spec(w_o.shape),
        ],
        out_specs=pl.BlockSpec((1, tm, d), lambda bi, ti: (bi, ti, 0)),
        out_shape=jax.ShapeDtypeStruct((b, t, d), F32),
        compiler_params=_cparams(("parallel", "parallel")),
    )(ctx, x, o, mods, w_o.astype(BF16))


def _halo_specs(tm, d, n_row_blocks):
    per = tm // HALO
    return [
        pl.BlockSpec((1, tm, d), lambda bi, ti: (bi, ti, 0)),
        pl.BlockSpec((1, HALO, d), lambda bi, ti: (bi, jnp.maximum(ti * per - 1, 0), 0)),
        pl.BlockSpec((1, HALO, d), lambda bi, ti: (bi, jnp.minimum((ti + 1) * per, n_row_blocks - 1), 0)),
    ]


def _edge_keep(ti, n_ctx_tiles, n_tiles):
    first = (ti == 0) | (ti == n_ctx_tiles)
    last = (ti == n_tiles - 1) | (ti == n_ctx_tiles - 1)
    return jnp.where(first, 0.0, 1.0).astype(F32), jnp.where(last, 0.0, 1.0).astype(F32)


def _ffn_tail(xp, xm, xn, mod, keep_prev, keep_next, wup_ref, cw_ref, cb_ref, wdn_ref, n_fc):
    tm = xm.shape[0]
    ext = tm + 2 * HALO
    shift, scale, gate = mod[3:4], mod[4:5], mod[5:6]
    h = jnp.concatenate([_modulate(xp, shift, scale) * keep_prev,
                         _modulate(xm, shift, scale),
                         _modulate(xn, shift, scale) * keep_next], axis=0).astype(BF16)

    def conv(u, cw, cb):
        y = cw[0:1] * pltpu.roll(u, 1, 0) + cw[1:2] * u + cw[2:3] * pltpu.roll(u, ext - 1, 0) + cb
        return y[HALO:HALO + tm]

    col = lambda j: slice(j * FF_CHUNK, (j + 1) * FF_CHUNK)
    up = lambda j: (jnp.dot(h, wup_ref[:, col(j)], preferred_element_type=F32),
                    jnp.dot(h, wup_ref[:, col(n_fc + j)], preferred_element_type=F32))
    acc = None
    ahead = [up(0), up(1)]
    for j in range(n_fc):
        ug, uv = ahead.pop(0)
        if j + 2 < n_fc:
            ahead.append(up(j + 2))
        cg = conv(ug, cw_ref[:, col(j)], cb_ref[:, col(j)])
        cv = conv(uv, cw_ref[:, col(n_fc + j)], cb_ref[:, col(n_fc + j)])
        act = cg * _sigmoid(cg) * cv
        down = jnp.dot(act.astype(BF16), wdn_ref[col(j), :], preferred_element_type=F32)
        acc = down if acc is None else acc + down
    return xm + gate * acc


def _ffn_kernel(xm_ref, xp_ref, xn_ref, mod_ref, wup_ref, cw_ref, cb_ref, wdn_ref, out_ref,
                *, n_ctx_tiles, n_tiles, n_fc):
    keep_prev, keep_next = _edge_keep(pl.program_id(1), n_ctx_tiles, n_tiles)
    out_ref[0] = _ffn_tail(xp_ref[0], xm_ref[0], xn_ref[0], mod_ref[0], keep_prev, keep_next,
                           wup_ref, cw_ref, cb_ref, wdn_ref, n_fc)


def _rwkv_out_ffn_kernel(x_ref, xp_ref, xn_ref, yf_ref, yfp_ref, yfn_ref, yb_ref, ybp_ref, ybn_ref,
                         bv_ref, bvp_ref, bvn_ref, g_ref, gp_ref, gn_ref, mod_ref, gnw_ref, gnb_ref, gmean_ref,
                         wo_ref, wup_ref, cw_ref, cb_ref, wdn_ref, fg_ref, out_ref, *, n_tiles, n_fc):
    tm = x_ref.shape[1]
    rows = lambda p, m, n: jnp.concatenate([p[0], m[0], n[0]], axis=0)
    mod = mod_ref[0]
    y = rows(yfp_ref, yf_ref, yfn_ref).astype(F32) + rows(ybp_ref, yb_ref, ybn_ref).astype(F32)
    gmean = gmean_ref[...]
    dev = y - _head_sum(y, gmean)
    var = _head_sum(dev * dev, gmean)
    y_n = dev * lax.rsqrt(var + GN_EPS) * gnw_ref[...] + gnb_ref[...]
    z = (y_n + rows(bvp_ref, bv_ref, bvn_ref)) * rows(gp_ref, g_ref, gn_ref)
    x1 = rows(xp_ref, x_ref, xn_ref) + mod[2:3] * _dot(z, wo_ref[...])
    keep_prev, keep_next = _edge_keep(pl.program_id(1), 0, n_tiles)
    lo = WIDE_HALO
    out = _ffn_tail(x1[lo - HALO:lo], x1[lo:lo + tm], x1[lo + tm:lo + tm + HALO], mod, keep_prev, keep_next,
                    wup_ref, cw_ref, cb_ref, wdn_ref, n_fc)
    out_ref[0] = _rms_norm(out) * fg_ref[...]


def _conv_ffn(xa, mods, w_up, conv_w, conv_b, w_down, n_ctx_tiles, tm):
    b, t, d = xa.shape
    n_fc = w_down.shape[0] // FF_CHUNK
    n_tiles = t // tm
    nb = b
    weights = [w_up.astype(BF16), conv_w, conv_b[None], w_down.astype(BF16)]
    mod_map = lambda bi, ti: (jnp.where(ti < n_ctx_tiles, nb, bi), 0, 0)
    return pl.pallas_call(
        functools.partial(_ffn_kernel, n_ctx_tiles=n_ctx_tiles, n_tiles=n_tiles, n_fc=n_fc),
        grid=(b, n_tiles),
        in_specs=_halo_specs(tm, d, t // HALO) + [pl.BlockSpec((1, N_MOD, d), mod_map)]
        + [_const_spec(w.shape) for w in weights],
        out_specs=pl.BlockSpec((1, tm, d), lambda bi, ti: (bi, ti, 0)),
        out_shape=jax.ShapeDtypeStruct((b, t, d), F32),
        compiler_params=_cparams(("parallel", "parallel")),
    )(xa, xa, xa, mods, *weights)


def _rwkv_out_conv_ffn(xa, y, bv, g, mods, gn_w, gn_b, w_o, w_up, conv_w, conv_b, w_down, final_gain, n_ctx_tiles):
    b, t, d = xa.shape
    tm = TOKEN_TILE
    n_tiles = t // tm - n_ctx_tiles
    n_fc = w_down.shape[0] // FF_CHUNK
    per = tm // WIDE_HALO
    n_halo_blocks = t // WIDE_HALO
    token_specs = [
        pl.BlockSpec((1, tm, d), lambda bi, ti: (bi, ti + n_ctx_tiles, 0)),
        pl.BlockSpec((1, WIDE_HALO, d), lambda bi, ti: (bi, jnp.maximum((ti + n_ctx_tiles) * per - 1, 0), 0)),
        pl.BlockSpec((1, WIDE_HALO, d),
                     lambda bi, ti: (bi, jnp.minimum((ti + n_ctx_tiles + 1) * per, n_halo_blocks - 1), 0)),
    ]
    consts = [gn_w[None], gn_b[None], _head_indicator(RWKV_HEAD, 1.0 / RWKV_HEAD), w_o.astype(BF16),
              w_up.astype(BF16), conv_w, conv_b[None], w_down.astype(BF16), final_gain[None]]
    tokens = [xa, y[0], y[1], bv, g]
    return pl.pallas_call(
        functools.partial(_rwkv_out_ffn_kernel, n_tiles=n_tiles, n_fc=n_fc),
        grid=(b, n_tiles),
        in_specs=token_specs * len(tokens) + [pl.BlockSpec((1, N_MOD, d), lambda bi, ti: (bi, 0, 0))]
        + [_const_spec(c.shape) for c in consts],
        out_specs=pl.BlockSpec((1, tm, d), lambda bi, ti: (bi, ti, 0)),
        out_shape=jax.ShapeDtypeStruct((b, n_tiles * tm, d), F32),
        compiler_params=_cparams(("parallel", "parallel")),
    )(*[a for a in tokens for _ in range(3)], mods, *consts)


def _rwkv_pre_kernel(xm_ref, xp_ref, xn_ref, mod_ref, mu_ref, wr_ref, wk_ref, wv_ref,
                     dw1_ref, dw2_ref, w0_ref, a1_ref, a2_ref, a0_ref, g1_ref, g2_ref,
                     kk_ref, ka_ref, rk_ref, gsum_ref,
                     r_out, v_out, k_out, kk_out, a_out, ld_out, g_out, bv_out,
                     *, n_ctx_tiles, n_tiles):
    tm = xm_ref.shape[1]
    d = xm_ref.shape[2]
    ext = tm + 2 * HALO
    mod = mod_ref[0]
    shift, scale = mod[0:1], mod[1:2]
    keep_prev, keep_next = _edge_keep(pl.program_id(1), n_ctx_tiles, n_tiles)
    h = _modulate(xm_ref[0], shift, scale)
    he = jnp.concatenate([_modulate(xp_ref[0], shift, scale) * keep_prev, h,
                          _modulate(xn_ref[0], shift, scale) * keep_next], axis=0)
    xx = 0.5 * (pltpu.roll(he, 1, 0) + pltpu.roll(he, ext - 1, 0))[HALO:HALO + tm] - h
    mu = mu_ref[...]
    mix = lambda i: h + xx * mu[i:i + 1]
    r = _dot(mix(0), wr_ref[...])
    k = _dot(mix(2), wk_ref[...])
    v = _dot(mix(3), wv_ref[...])
    z = w0_ref[...] + _dot(jnp.tanh(_dot(mix(1), dw1_ref[...])), dw2_ref[...])
    log_decay = -math.exp(-0.5) * _sigmoid(z)
    a = _sigmoid(a0_ref[...] + _dot(_dot(mix(4), a1_ref[...]), a2_ref[...]))
    g = _dot(_sigmoid(_dot(mix(5), g1_ref[...])), g2_ref[...])
    gsum = gsum_ref[...]
    kx = k * kk_ref[...]
    kk = kx / jnp.maximum(jnp.sqrt(_head_sum(kx * kx, gsum)), 1e-12)
    ka = ka_ref[...]
    bonus = (r * rk_ref[...]) * k * (2.0 + (a[:, :d] + a[:, d:] - 2.0) * ka)
    r_out[0] = r
    v_out[0] = v.astype(BF16)
    k_out[0] = k
    kk_out[0] = kk
    a_out[0, 0] = a[:, :d]
    a_out[1, 0] = a[:, d:]
    ld_out[0, 0] = log_decay[:, :d]
    ld_out[1, 0] = log_decay[:, d:]
    g_out[0] = g.astype(BF16)
    bv_out[0] = (_head_sum(bonus, gsum) * v).astype(BF16)


def _block_diag2(m0, m1):
    z = jnp.zeros_like(m0)
    return jnp.concatenate([jnp.concatenate([m0, z], axis=1), jnp.concatenate([z, m1], axis=1)], axis=0)


def _rwkv_pre(xa, mods, p, n_ctx_tiles):
    b, t, d = xa.shape
    tm = RWKV_TILE
    n_tiles = t // tm
    nb = b
    n_ctx = n_ctx_tiles * (TOKEN_TILE // tm)
    mod_map = lambda bi, ti: (jnp.where(ti < n_ctx, nb, bi), 0, 0)
    gate_pad = -p['gate_g1'].shape[1] % LANES
    consts = [
        p['mu'],
        p['w_r'].astype(BF16), p['w_k'].astype(BF16), p['w_v'].astype(BF16),
        jnp.concatenate([p['decay_w1'][0], p['decay_w1'][1]], axis=1).astype(BF16),
        _block_diag2(p['decay_w2'][0], p['decay_w2'][1]).astype(BF16),
        p['decay_w0'].reshape(1, 2 * d),
        jnp.concatenate([p['aaa_a1'][0], p['aaa_a1'][1]], axis=1).astype(BF16),
        _block_diag2(p['aaa_a2'][0], p['aaa_a2'][1]).astype(BF16),
        p['aaa_a0'].reshape(1, 2 * d),
        jnp.pad(p['gate_g1'], ((0, 0), (0, gate_pad))).astype(BF16),
        jnp.pad(p['gate_g2'], ((0, gate_pad), (0, 0))).astype(BF16),
        p['k_k'][None], p['k_a'][None], p['r_k'].reshape(1, d),
        _head_indicator(RWKV_HEAD, 1.0),
    ]
    tok = pl.BlockSpec((1, tm, d), lambda bi, ti: (bi, ti, 0))
    tok2 = pl.BlockSpec((2, 1, tm, d), lambda bi, ti: (0, bi, ti, 0))
    s1 = jax.ShapeDtypeStruct((b, t, d), F32)
    s2 = jax.ShapeDtypeStruct((2, b, t, d), F32)
    h1 = jax.ShapeDtypeStruct((b, t, d), BF16)
    return pl.pallas_call(
        functools.partial(_rwkv_pre_kernel, n_ctx_tiles=n_ctx, n_tiles=n_tiles),
        grid=(b, n_tiles),
        in_specs=_halo_specs(tm, d, t // HALO) + [pl.BlockSpec((1, N_MOD, d), mod_map)]
        + [_const_spec(c.shape) for c in consts],
        out_specs=[tok, tok, tok, tok, tok2, tok2, tok, tok],
        out_shape=[s1, h1, s1, s1, s2, s2, h1, h1],
        compiler_params=_cparams(("parallel", "parallel")),
    )(xa, xa, xa, mods, *consts)


def _head_stack(x, lane_head):
    return jnp.concatenate([jnp.where(lane_head == hh, x, 0.0) for hh in range(HEADS_PER_GROUP)], axis=0)


def _wkv_kernel(rf_ref, vf_ref, kf_ref, kkf_ref, af_ref, ldf_ref, rb_ref, vb_ref, kb_ref, kkb_ref, ab_ref, ldb_ref,
                ka_ref, yf_ref, yb_ref, state_ref):
    gw = HEADS_PER_GROUP * RWKV_HEAD
    n_groups = rf_ref.shape[2] // gw

    @pl.when(pl.program_id(1) == 0)
    def _():
        state_ref[...] = jnp.zeros(state_ref.shape, F32)

    row = lax.broadcasted_iota(jnp.int32, (CHUNK, gw), 0)
    lane = lax.broadcasted_iota(jnp.int32, (CHUNK, gw), 1)
    col = lane % CHUNK
    lane_head = lane // RWKV_HEAD
    eye = (row == col).astype(F32)
    bd_r = lax.broadcasted_iota(jnp.int32, (gw, gw), 0) // RWKV_HEAD
    bd_c = lax.broadcasted_iota(jnp.int32, (gw, gw), 1) // RWKV_HEAD
    block_diag = bd_r == bd_c
    ka = ka_ref[...]
    stack = lambda x: _head_stack(x, lane_head)
    cols = [slice(gi * gw, (gi + 1) * gw) for gi in range(n_groups)]

    n_sub = rf_ref.shape[1] // CHUNK
    wide = {}
    for di, (refs, incl, last) in enumerate((((rf_ref, vf_ref, kf_ref, kkf_ref, af_ref, ldf_ref), row >= col, CHUNK - 1),
                                             ((rb_ref, vb_ref, kb_ref, kkb_ref, ab_ref, ldb_ref), row <= col, 0))):
        r_ref, v_ref, k_ref, kk_ref, a_ref, ld_ref = refs
        tri = jnp.where(incl[:, :CHUNK], 1.0, 0.0).astype(BF16)
        for wi in range(n_sub):
            ci = wi if di == 0 else n_sub - 1 - wi
            rows = slice(ci * CHUNK, (ci + 1) * CHUNK)
            ld = ld_ref[0, 0, rows]
            ag = a_ref[0, 0, rows]
            kk = kk_ref[0, rows]
            kc = k_ref[0, rows] * (1.0 + (ag - 1.0) * ka)
            bc = kk * ag
            log_p = _dot_split_lhs_rhs(tri, ld)
            log_p_last = log_p[last:last + 1]
            dec_out = jnp.exp(-log_p)
            dec_tail = jnp.exp(log_p_last - log_p)
            wide[di, wi] = dict(
                rows=rows, incl=incl, strict=incl & (row != col), vc=v_ref[0, rows].astype(F32),
                ar=jnp.concatenate([-kk * jnp.exp(log_p - ld), r_ref[0, rows] * jnp.exp(log_p)], axis=0),
                b_out=bc * dec_out, k_out=kc * dec_out,
                bk=jnp.concatenate([bc * dec_tail, kc * dec_tail], axis=0),
                state_decay=jnp.exp(log_p_last))

    chains = [(di, wi, gi) for wi in range(n_sub) for di in range(2) for gi in range(n_groups)]
    w = lambda ch, name: wide[ch[0], ch[1]][name]
    part = lambda ch, name: wide[ch[0], ch[1]][name][:, cols[ch[2]]]
    xb = [_dot_nt(part(ch, 'ar'), stack(part(ch, 'b_out'))) for ch in chains]
    xk = [_dot_nt(part(ch, 'ar'), stack(part(ch, 'k_out'))) for ch in chains]
    a_ab = [jnp.where(w(ch, 'strict'), x[:CHUNK], 0.0) for ch, x in zip(chains, xb)]
    a_rb = [jnp.where(w(ch, 'incl'), x[CHUNK:], 0.0) for ch, x in zip(chains, xb)]
    a_kk = [jnp.concatenate([jnp.where(w(ch, 'strict'), x[:CHUNK], 0.0), jnp.where(w(ch, 'incl'), x[CHUNK:], 0.0)],
                            axis=0) for ch, x in zip(chains, xk)]
    av = [_dot(a, stack(part(ch, 'vc'))) for ch, a in zip(chains, a_kk)]

    m = [eye + jnp.where(row // 2 == col // 2, a, 0.0) for a in a_ab]
    s = 2
    while s < CHUNK:
        couples = (row // (2 * s) == col // (2 * s)) & (row // s != col // s)
        t_off = [_dot(jnp.where(couples, a, 0.0), stack(mm)) for a, mm in zip(a_ab, m)]
        m = [mm + _dot(mm, stack(t)) for mm, t in zip(m, t_off)]
        s *= 2

    state = {(di, gi): state_ref[di * n_groups + gi] for di in range(2) for gi in range(n_groups)}
    for wi in range(n_sub):
        idx = [i for i, ch in enumerate(chains) if ch[1] == wi]
        arh = {i: _dot_nt(part(chains[i], 'ar'), state[chains[i][0], chains[i][2]]) for i in idx}
        u = {i: _dot(m[i], stack(arh[i][:CHUNK] + av[i][:CHUNK])) for i in idx}
        y = {i: arh[i][CHUNK:] + _dot(a_rb[i], stack(u[i])) + av[i][CHUNK:] for i in idx}
        upd = {i: _dot_tn(jnp.concatenate([u[i], part(chains[i], 'vc')], axis=0), part(chains[i], 'bk')) for i in idx}
        for di, y_ref in ((0, yf_ref), (1, yb_ref)):
            y_ref[0, wide[di, wi]['rows']] = jnp.concatenate([y[i] for i in idx if chains[i][0] == di],
                                                             axis=1).astype(y_ref.dtype)
        for i in idx:
            di, _, gi = chains[i]
            state[di, gi] = part(chains[i], 'state_decay') * state[di, gi] + jnp.where(block_diag, upd[i], 0.0)
    for (di, gi), s in state.items():
        state_ref[di * n_groups + gi] = s


def _dot_split_lhs_rhs(tri_bf16, x):
    hi, lo = _split(x)
    return (jnp.dot(tri_bf16, hi, preferred_element_type=F32)
            + jnp.dot(tri_bf16, lo, preferred_element_type=F32))


def _wkv(r, v, k, kk, a, ld, k_a, ctx_len):
    b, t, d = r.shape
    rows = WKV_STEP_CHUNKS * CHUNK
    assert ctx_len % rows == 0 and t % rows == 0
    n_steps = t // rows
    n_ctx = ctx_len // rows

    def back(ci):
        return jnp.where(ci < n_ctx, n_ctx - 1 - ci, n_steps - 1 - (ci - n_ctx))

    fwd = pl.BlockSpec((1, rows, d), lambda bi, ci: (bi, ci, 0))
    bwd = pl.BlockSpec((1, rows, d), lambda bi, ci: (bi, back(ci), 0))
    fwd2 = pl.BlockSpec((1, 1, rows, d), lambda bi, ci: (0, bi, ci, 0))
    bwd2 = pl.BlockSpec((1, 1, rows, d), lambda bi, ci: (1, bi, back(ci), 0))
    gw = HEADS_PER_GROUP * RWKV_HEAD
    y_shape = jax.ShapeDtypeStruct((b, t, d), BF16)
    return pl.pallas_call(
        _wkv_kernel,
        grid=(b, n_steps),
        in_specs=[fwd, fwd, fwd, fwd, fwd2, fwd2, bwd, bwd, bwd, bwd, bwd2, bwd2, _const_spec((1, d))],
        out_specs=[fwd, bwd],
        out_shape=[y_shape, y_shape],
        scratch_shapes=[pltpu.VMEM((2 * d // gw, gw, gw), F32)],
        compiler_params=_cparams(("arbitrary", "arbitrary")),
    )(r, v, k, kk, a, ld, r, v, k, kk, a, ld, k_a[None])


def _rope_tables(seq, ctx_len):
    rows = seq // GRID_W
    row = jnp.repeat(jnp.arange(rows, dtype=F32), GRID_W)
    col = jnp.tile(jnp.arange(GRID_W, dtype=F32), rows)
    n_freq = HEAD_DIM // 4
    inv_freq = ROPE_THETA ** (-jnp.arange(n_freq, dtype=F32) / n_freq)
    ang = jnp.concatenate([row[:, None] * inv_freq, col[:, None] * inv_freq], axis=-1)
    cos = jnp.repeat(jnp.cos(ang), 2, axis=1)
    sin = jnp.repeat(jnp.sin(ang), 2, axis=1) * jnp.tile(jnp.array([-1.0, 1.0], F32), HEAD_DIM // 2)
    cos = jnp.concatenate([jnp.ones((ctx_len, HEAD_DIM), F32), cos], axis=0)
    sin = jnp.concatenate([jnp.zeros((ctx_len, HEAD_DIM), F32), sin], axis=0)
    reps = LANES // HEAD_DIM
    return jnp.tile(cos, (1, reps)), jnp.tile(sin, (1, reps))


def kernel(x, c, ctx, c_ctx, ada_w, ada_b, attn_w_qkv, attn_q_gain, attn_k_gain, attn_w_o, rwkv_mu, rwkv_w_r, rwkv_w_k, rwkv_w_v, rwkv_w_o, rwkv_decay_w0, rwkv_decay_w1, rwkv_decay_w2, rwkv_aaa_a0, rwkv_aaa_a1, rwkv_aaa_a2, rwkv_gate_g1, rwkv_gate_g2, rwkv_k_k, rwkv_k_a, rwkv_r_k, rwkv_gn_w, rwkv_gn_b, ffn_w_up, ffn_conv_w, ffn_conv_b, ffn_w_down, final_gain):
    b, seq, d = x.shape
    ctx_len = ctx.shape[1]
    assert ada_w.shape[0] == 2 and attn_w_qkv.shape[0] == 1 and rwkv_w_r.shape[0] == 1
    assert b + 1 <= MOD_ROWS and ctx_len % TOKEN_TILE == 0 and seq % TOKEN_TILE == 0 and seq % GRID_W == 0
    n_ctx_tiles = ctx_len // TOKEN_TILE

    cc = jnp.zeros((MOD_ROWS, d), F32).at[:b].set(c).at[b].set(c_ctx)
    mods = _ada_mods(cc, ada_w, ada_b)

    cos_t, sin_t = _rope_tables(seq, ctx_len)
    q, kt, v = _attn_pre(ctx, x, mods[0], attn_w_qkv[0], attn_q_gain[0], attn_k_gain[0], cos_t, sin_t, n_ctx_tiles)
    o = _flash(q, kt, v, ctx_len)
    xa = _attn_out(ctx, x, o, mods[0], attn_w_o[0], n_ctx_tiles)
    xa = _conv_ffn(xa, mods[0], ffn_w_up[0], ffn_conv_w[0], ffn_conv_b[0], ffn_w_down[0], n_ctx_tiles, TOKEN_TILE)

    p = dict(mu=rwkv_mu[0], w_r=rwkv_w_r[0], w_k=rwkv_w_k[0], w_v=rwkv_w_v[0],
             decay_w0=rwkv_decay_w0[0], decay_w1=rwkv_decay_w1[0], decay_w2=rwkv_decay_w2[0],
             aaa_a0=rwkv_aaa_a0[0], aaa_a1=rwkv_aaa_a1[0], aaa_a2=rwkv_aaa_a2[0],
             gate_g1=rwkv_gate_g1[0], gate_g2=rwkv_gate_g2[0],
             k_k=rwkv_k_k[0], k_a=rwkv_k_a[0], r_k=rwkv_r_k[0])
    r, vv, k, kk, a, ld, g, bv = _rwkv_pre(xa, mods[1], p, n_ctx_tiles)
    y = _wkv(r, vv, k, kk, a, ld, rwkv_k_a[0], ctx_len)
    return _rwkv_out_conv_ffn(xa, y, bv, g, mods[1], rwkv_gn_w[0], rwkv_gn_b[0], rwkv_w_o[0],
                              ffn_w_up[1], ffn_conv_w[1], ffn_conv_b[1], ffn_w_down[1], final_gain, n_ctx_tiles)
```

```python
import functools
import math

import jax
import jax.numpy as jnp
from jax import lax
from jax.experimental import pallas as pl
from jax.experimental.pallas import tpu as pltpu

F32 = jnp.float32
BF16 = jnp.bfloat16

LANES = 128
SUBLANES = 8
MXU_WIDTH = 256
VMEM_LIMIT_BYTES = 56 * 1024 * 1024

N_HEADS = 16
N_KV_HEADS = 4
HEAD_DIM = 64
GROUP = N_HEADS // N_KV_HEADS
RWKV_HEAD = 64
GRID_W = 64
ROPE_THETA = 10000.0
NORM_EPS = 1e-6
GN_EPS = 64e-5
N_MOD = 6
MOD_ROWS = 8

TOKEN_TILE = 256
RWKV_TILE = 256
Q_TILE = 256
KV_TILE = 2816
KV_HEADS_PER_STEP = 1
FF_CHUNK = 256
CHUNK = 64
HEADS_PER_GROUP = 4
WKV_STEP_CHUNKS = 4
HALO = SUBLANES
WIDE_HALO = 2 * SUBLANES


def _cparams(sem):
    return pltpu.CompilerParams(dimension_semantics=sem, vmem_limit_bytes=VMEM_LIMIT_BYTES)


def _dot(a, b):
    return jnp.dot(a.astype(BF16), b.astype(BF16), preferred_element_type=F32)


def _dot_nt(a, b):
    return lax.dot_general(a.astype(BF16), b.astype(BF16), (((1,), (1,)), ((), ())),
                           preferred_element_type=F32)


def _dot_tn(a, b):
    return lax.dot_general(a.astype(BF16), b.astype(BF16), (((0,), (0,)), ((), ())),
                           preferred_element_type=F32)


def _split(a):
    hi = a.astype(BF16)
    lo = (a - hi.astype(F32)).astype(BF16)
    return hi, lo


def _head_indicator(head_dim, scale):
    return jnp.kron(jnp.eye(MXU_WIDTH // head_dim, dtype=F32), jnp.full((head_dim, head_dim), scale, F32)).astype(BF16)


def _head_sum(a, indicator):
    a = a.astype(BF16)
    blocks = [jnp.dot(a[:, j:j + MXU_WIDTH], indicator, preferred_element_type=F32)
              for j in range(0, a.shape[1], MXU_WIDTH)]
    return blocks[0] if len(blocks) == 1 else jnp.concatenate(blocks, axis=1)


def _dot_split_both(a, b):
    ah, al = _split(a)
    bh, bl = _split(b)
    d = functools.partial(jnp.dot, preferred_element_type=F32)
    return d(ah, bh) + (d(ah, bl) + d(al, bh))


def _sigmoid(x):
    return 1.0 / (1.0 + jnp.exp(-x))


def _rms_norm(x):
    return x * lax.rsqrt(jnp.mean(x * x, axis=-1, keepdims=True) + NORM_EPS)


def _modulate(x, shift, scale):
    return _rms_norm(x) * (1.0 + scale) + shift


def _const_spec(shape):
    nd = len(shape)
    return pl.BlockSpec(shape, lambda *_: (0,) * nd)


def _ada_kernel(c_ref, w_ref, b_ref, o_ref):
    cc = c_ref[...]
    o_ref[0, 0] = _dot_split_both(cc * _sigmoid(cc), w_ref[0]) + b_ref[0, 0]


def _ada_mods(cc, ada_w, ada_b):
    depth, d, _ = ada_w.shape
    out = pl.pallas_call(
        _ada_kernel,
        grid=(depth, N_MOD),
        in_specs=[
            pl.BlockSpec((MOD_ROWS, d), lambda i, n: (0, 0)),
            pl.BlockSpec((1, d, d), lambda i, n: (i, 0, n)),
            pl.BlockSpec((1, 1, 1, d), lambda i, n: (i, n, 0, 0)),
        ],
        out_specs=pl.BlockSpec((1, 1, MOD_ROWS, d), lambda i, n: (i, n, 0, 0)),
        out_shape=jax.ShapeDtypeStruct((depth, N_MOD, MOD_ROWS, d), F32),
        compiler_params=_cparams(("arbitrary", "arbitrary")),
    )(cc, ada_w, ada_b.reshape(depth, N_MOD, 1, d))
    return out.transpose(0, 2, 1, 3)


def _rope(x, cos, sin):
    width = x.shape[1]
    reps = width // LANES
    cc = jnp.concatenate([cos] * reps, axis=1)
    ss = jnp.concatenate([sin] * reps, axis=1)
    lane = lax.broadcasted_iota(jnp.int32, x.shape, 1)
    partner = jnp.where((lane & 1) == 0, pltpu.roll(x, width - 1, 1), pltpu.roll(x, 1, 1))
    return x * cc + partner * ss


def _token_tile_specs(tm, d, n_ctx_tiles):
    return [pl.BlockSpec((1, tm, d), lambda bi, ti: (bi, jnp.minimum(ti, n_ctx_tiles - 1), 0)),
            pl.BlockSpec((1, tm, d), lambda bi, ti: (bi, jnp.maximum(ti - n_ctx_tiles, 0), 0))]


def _token_tile(c_ref, x_ref, n_ctx_tiles):
    return jnp.where(pl.program_id(1) < n_ctx_tiles, c_ref[0], x_ref[0])


def _attn_pre_kernel(c_ref, x_ref, mod_ref, w_ref, qg_ref, kg_ref, hmean_ref, cos_ref, sin_ref,
                     q_ref, kt_ref, v_ref, *, q_dim, kv_dim, n_ctx_tiles):
    mod = mod_ref[0]
    h = _modulate(_token_tile(c_ref, x_ref, n_ctx_tiles), mod[0:1], mod[1:2])
    qkv = _dot(h, w_ref[...])
    q = qkv[:, :q_dim]
    k = qkv[:, q_dim:q_dim + kv_dim]
    v = qkv[:, q_dim + kv_dim:]
    hmean = hmean_ref[...]
    q = q * lax.rsqrt(_head_sum(q * q, hmean) + NORM_EPS) * qg_ref[...]
    k = k * lax.rsqrt(_head_sum(k * k, hmean) + NORM_EPS) * kg_ref[...]
    cos = cos_ref[...]
    sin = sin_ref[...]
    q = _rope(q, cos, sin) * (HEAD_DIM ** -0.5 * math.log2(math.e))
    k = _rope(k, cos, sin)
    q_ref[0] = q.astype(BF16)
    kt_ref[0] = k.T.astype(BF16)
    rows = v.shape[0]
    one_col = (lax.broadcasted_iota(jnp.int32, (rows, HEAD_DIM), 1) == 0).astype(F32)
    for g in range(N_KV_HEADS):
        v_ref[0, g] = jnp.concatenate([v[:, g * HEAD_DIM:(g + 1) * HEAD_DIM], one_col], axis=1).astype(BF16)


def _attn_pre(ctx, x, mods, w_qkv, q_gain, k_gain, cos_t, sin_t, n_ctx_tiles):
    b, seq, d = x.shape
    t = ctx.shape[1] + seq
    q_dim = N_HEADS * HEAD_DIM
    kv_dim = N_KV_HEADS * HEAD_DIM
    tm = TOKEN_TILE
    nb = b
    mod_map = lambda bi, ti: (jnp.where(ti < n_ctx_tiles, nb, bi), 0, 0)
    return pl.pallas_call(
        functools.partial(_attn_pre_kernel, q_dim=q_dim, kv_dim=kv_dim, n_ctx_tiles=n_ctx_tiles),
        grid=(b, t // tm),
        in_specs=_token_tile_specs(tm, d, n_ctx_tiles) + [
            pl.BlockSpec((1, N_MOD, d), mod_map),
            _const_spec((d, q_dim + 2 * kv_dim)),
            _const_spec((1, q_dim)),
            _const_spec((1, kv_dim)),
            _const_spec((MXU_WIDTH, MXU_WIDTH)),
            pl.BlockSpec((tm, LANES), lambda bi, ti: (ti, 0)),
            pl.BlockSpec((tm, LANES), lambda bi, ti: (ti, 0)),
        ],
        out_specs=[
            pl.BlockSpec((1, tm, q_dim), lambda bi, ti: (bi, ti, 0)),
            pl.BlockSpec((1, kv_dim, tm), lambda bi, ti: (bi, 0, ti)),
            pl.BlockSpec((1, N_KV_HEADS, tm, LANES), lambda bi, ti: (bi, 0, ti, 0)),
        ],
        out_shape=[
            jax.ShapeDtypeStruct((b, t, q_dim), BF16),
            jax.ShapeDtypeStruct((b, kv_dim, t), BF16),
            jax.ShapeDtypeStruct((b, N_KV_HEADS, t, LANES), BF16),
        ],
        compiler_params=_cparams(("parallel", "parallel")),
    )(ctx, x, mods, w_qkv.astype(BF16), jnp.tile(q_gain, N_HEADS)[None], jnp.tile(k_gain, N_KV_HEADS)[None],
      _head_indicator(HEAD_DIM, 1.0 / HEAD_DIM), cos_t, sin_t)


def _flash_kernel(q_ref, kt_ref, v_ref, o_ref, q4_ref, *, n_ctx_q, ctx_len, kv_tiles):
    qi = pl.program_id(2)
    tq = q_ref.shape[1]
    n_heads = q_ref.shape[2] // HEAD_DIM
    q = q_ref[0]
    for h in range(n_heads):
        q4_ref[h * tq:(h + 1) * tq] = q[:, h * HEAD_DIM:(h + 1) * HEAD_DIM]

    def attend(tiles):
        steps = [(start, size, g * GROUP + j) for g in range(n_heads // GROUP) for start, size in tiles
                 for j in range(GROUP)]

        def scores(st):
            start, size, h = st
            g = h // GROUP
            return jnp.dot(q4_ref[h * tq:(h + 1) * tq], kt_ref[0, g * HEAD_DIM:(g + 1) * HEAD_DIM, start:start + size],
                           preferred_element_type=F32)

        m = [None] * n_heads
        acc = [None] * n_heads
        s_next = scores(steps[0])
        for i, (start, size, h) in enumerate(steps):
            s = s_next
            if i + 1 < len(steps):
                s_next = scores(steps[i + 1])
            cols = [s[:, t * LANES:(t + 1) * LANES] for t in range(size // LANES)]
            m_cur = jnp.max(functools.reduce(jnp.maximum, cols), axis=1, keepdims=True)
            m_new = jnp.broadcast_to(m_cur, (tq, LANES)) if m[h] is None else jnp.maximum(m[h], m_cur)
            p = jnp.concatenate([jnp.exp2(c - m_new) for c in cols], axis=1).astype(BF16)
            pv = jnp.dot(p, v_ref[0, h // GROUP, start:start + size, :], preferred_element_type=F32)
            acc[h] = pv if acc[h] is None else jnp.exp2(m[h] - m_new) * acc[h] + pv
            m[h] = m_new
        o = [a[:, :HEAD_DIM] / a[:, HEAD_DIM:HEAD_DIM + 1] for a in acc]
        o_ref[0] = jnp.concatenate(o, axis=1).astype(BF16)

    @pl.when(qi < n_ctx_q)
    def _():
        attend([(0, ctx_len)])

    @pl.when(qi >= n_ctx_q)
    def _():
        attend(kv_tiles)


def _flash(q, kt, v, ctx_len):
    b, t, q_dim = q.shape
    tq = Q_TILE
    kvh = KV_HEADS_PER_STEP
    gw = kvh * GROUP * HEAD_DIM
    n_kv = next(n for n in range(1, t // LANES + 1) if t % (n * LANES) == 0 and t // n <= KV_TILE)
    kv_tiles = [(i * (t // n_kv), t // n_kv) for i in range(n_kv)]
    kern = functools.partial(_flash_kernel, n_ctx_q=ctx_len // tq, ctx_len=ctx_len, kv_tiles=kv_tiles)
    return pl.pallas_call(
        kern,
        grid=(b, N_KV_HEADS // kvh, t // tq),
        in_specs=[
            pl.BlockSpec((1, tq, gw), lambda bi, g, qi: (bi, qi, g)),
            pl.BlockSpec((1, kvh * HEAD_DIM, t), lambda bi, g, qi: (bi, g, 0)),
            pl.BlockSpec((1, kvh, t, LANES), lambda bi, g, qi: (bi, g, 0, 0)),
        ],
        out_specs=pl.BlockSpec((1, tq, gw), lambda bi, g, qi: (bi, qi, g)),
        out_shape=jax.ShapeDtypeStruct((b, t, q_dim), BF16),
        scratch_shapes=[pltpu.VMEM((kvh * GROUP * tq, HEAD_DIM), BF16)],
        compiler_params=_cparams(("parallel", "parallel", "arbitrary")),
    )(q, kt, v)


def _attn_out_kernel(c_ref, x_ref, o_ref, mod_ref, w_ref, out_ref, *, n_ctx_tiles):
    y = jnp.dot(o_ref[0], w_ref[...], preferred_element_type=F32)
    out_ref[0] = _token_tile(c_ref, x_ref, n_ctx_tiles) + mod_ref[0][2:3] * y


def _attn_out(ctx, x, o, mods, w_o, n_ctx_tiles):
    b, seq, d = x.shape
    t = ctx.shape[1] + seq
    tm = TOKEN_TILE
    nb = b
    mod_map = lambda bi, ti: (jnp.where(ti < n_ctx_tiles, nb, bi), 0, 0)
    return pl.pallas_call(
        functools.partial(_attn_out_kernel, n_ctx_tiles=n_ctx_tiles),
        grid=(b, t // tm),
        in_specs=_token_tile_specs(tm, d, n_ctx_tiles) + [
            pl.BlockSpec((1, tm, o.shape[2]), lambda bi, ti: (bi, ti, 0)),
            pl.BlockSpec((1, N_MOD, d), mod_map),
            _const_spec(w_o.shape),
        ],
        out_specs=pl.BlockSpec((1, tm, d), lambda bi, ti: (bi, ti, 0)),
        out_shape=jax.ShapeDtypeStruct((b, t, d), F32),
        compiler_params=_cparams(("parallel", "parallel")),
    )(ctx, x, o, mods, w_o.astype(BF16))


def _halo_specs(tm, d, n_row_blocks):
    per = tm // HALO
    return [
        pl.BlockSpec((1, tm, d), lambda bi, ti: (bi, ti, 0)),
        pl.BlockSpec((1, HALO, d), lambda bi, ti: (bi, jnp.maximum(ti * per - 1, 0), 0)),
        pl.BlockSpec((1, HALO, d), lambda bi, ti: (bi, jnp.minimum((ti + 1) * per, n_row_blocks - 1), 0)),
    ]


def _edge_keep(ti, n_ctx_tiles, n_tiles):
    first = (ti == 0) | (ti == n_ctx_tiles)
    last = (ti == n_tiles - 1) | (ti == n_ctx_tiles - 1)
    return jnp.where(first, 0.0, 1.0).astype(F32), jnp.where(last, 0.0, 1.0).astype(F32)


def _ffn_tail(xp, xm, xn, mod, keep_prev, keep_next, wup_ref, cw_ref, cb_ref, wdn_ref, n_fc):
    tm = xm.shape[0]
    ext = tm + 2 * HALO
    shift, scale, gate = mod[3:4], mod[4:5], mod[5:6]
    h = jnp.concatenate([_modulate(xp, shift, scale) * keep_prev,
                         _modulate(xm, shift, scale),
                         _modulate(xn, shift, scale) * keep_next], axis=0).astype(BF16)

    def conv(u, cw, cb):
        y = cw[0:1] * pltpu.roll(u, 1, 0) + cw[1:2] * u + cw[2:3] * pltpu.roll(u, ext - 1, 0) + cb
        return y[HALO:HALO + tm]

    col = lambda j: slice(j * FF_CHUNK, (j + 1) * FF_CHUNK)
    up = lambda j: (jnp.dot(h, wup_ref[:, col(j)], preferred_element_type=F32),
                    jnp.dot(h, wup_ref[:, col(n_fc + j)], preferred_element_type=F32))
    acc = None
    ahead = [up(0), up(1)]
    for j in range(n_fc):
        ug, uv = ahead.pop(0)
        if j + 2 < n_fc:
            ahead.append(up(j + 2))
        cg = conv(ug, cw_ref[:, col(j)], cb_ref[:, col(j)])
        cv = conv(uv, cw_ref[:, col(n_fc + j)], cb_ref[:, col(n_fc + j)])
        act = cg * _sigmoid(cg) * cv
        down = jnp.dot(act.astype(BF16), wdn_ref[col(j), :], preferred_element_type=F32)
        acc = down if acc is None else acc + down
    return xm + gate * acc


def _ffn_kernel(xm_ref, xp_ref, xn_ref, mod_ref, wup_ref, cw_ref, cb_ref, wdn_ref, out_ref,
                *, n_ctx_tiles, n_tiles, n_fc):
    keep_prev, keep_next = _edge_keep(pl.program_id(1), n_ctx_tiles, n_tiles)
    out_ref[0] = _ffn_tail(xp_ref[0], xm_ref[0], xn_ref[0], mod_ref[0], keep_prev, keep_next,
                           wup_ref, cw_ref, cb_ref, wdn_ref, n_fc)


def _rwkv_out_ffn_kernel(x_ref, xp_ref, xn_ref, yf_ref, yfp_ref, yfn_ref, yb_ref, ybp_ref, ybn_ref,
                         bv_ref, bvp_ref, bvn_ref, g_ref, gp_ref, gn_ref, mod_ref, gnw_ref, gnb_ref, gmean_ref,
                         wo_ref, wup_ref, cw_ref, cb_ref, wdn_ref, fg_ref, out_ref, *, n_tiles, n_fc):
    tm = x_ref.shape[1]
    rows = lambda p, m, n: jnp.concatenate([p[0], m[0], n[0]], axis=0)
    mod = mod_ref[0]
    y = rows(yfp_ref, yf_ref, yfn_ref).astype(F32) + rows(ybp_ref, yb_ref, ybn_ref).astype(F32)
    gmean = gmean_ref[...]
    dev = y - _head_sum(y, gmean)
    var = _head_sum(dev * dev, gmean)
    y_n = dev * lax.rsqrt(var + GN_EPS) * gnw_ref[...] + gnb_ref[...]
    z = (y_n + rows(bvp_ref, bv_ref, bvn_ref)) * rows(gp_ref, g_ref, gn_ref)
    x1 = rows(xp_ref, x_ref, xn_ref) + mod[2:3] * _dot(z, wo_ref[...])
    keep_prev, keep_next = _edge_keep(pl.program_id(1), 0, n_tiles)
    lo = WIDE_HALO
    out = _ffn_tail(x1[lo - HALO:lo], x1[lo:lo + tm], x1[lo + tm:lo + tm + HALO], mod, keep_prev, keep_next,
                    wup_ref, cw_ref, cb_ref, wdn_ref, n_fc)
    out_ref[0] = _rms_norm(out) * fg_ref[...]


def _conv_ffn(xa, mods, w_up, conv_w, conv_b, w_down, n_ctx_tiles, tm):
    b, t, d = xa.shape
    n_fc = w_down.shape[0] // FF_CHUNK
    n_tiles = t // tm
    nb = b
    weights = [w_up.astype(BF16), conv_w, conv_b[None], w_down.astype(BF16)]
    mod_map = lambda bi, ti: (jnp.where(ti < n_ctx_tiles, nb, bi), 0, 0)
    return pl.pallas_call(
        functools.partial(_ffn_kernel, n_ctx_tiles=n_ctx_tiles, n_tiles=n_tiles, n_fc=n_fc),
        grid=(b, n_tiles),
        in_specs=_halo_specs(tm, d, t // HALO) + [pl.BlockSpec((1, N_MOD, d), mod_map)]
        + [_const_spec(w.shape) for w in weights],
        out_specs=pl.BlockSpec((1, tm, d), lambda bi, ti: (bi, ti, 0)),
        out_shape=jax.ShapeDtypeStruct((b, t, d), F32),
        compiler_params=_cparams(("parallel", "parallel")),
    )(xa, xa, xa, mods, *weights)


def _rwkv_out_conv_ffn(xa, y, bv, g, mods, gn_w, gn_b, w_o, w_up, conv_w, conv_b, w_down, final_gain, n_ctx_tiles):
    b, t, d = xa.shape
    tm = TOKEN_TILE
    n_tiles = t // tm - n_ctx_tiles
    n_fc = w_down.shape[0] // FF_CHUNK
    per = tm // WIDE_HALO
    n_halo_blocks = t // WIDE_HALO
    token_specs = [
        pl.BlockSpec((1, tm, d), lambda bi, ti: (bi, ti + n_ctx_tiles, 0)),
        pl.BlockSpec((1, WIDE_HALO, d), lambda bi, ti: (bi, jnp.maximum((ti + n_ctx_tiles) * per - 1, 0), 0)),
        pl.BlockSpec((1, WIDE_HALO, d),
                     lambda bi, ti: (bi, jnp.minimum((ti + n_ctx_tiles + 1) * per, n_halo_blocks - 1), 0)),
    ]
    consts = [gn_w[None], gn_b[None], _head_indicator(RWKV_HEAD, 1.0 / RWKV_HEAD), w_o.astype(BF16),
              w_up.astype(BF16), conv_w, conv_b[None], w_down.astype(BF16), final_gain[None]]
    tokens = [xa, y[0], y[1], bv, g]
    return pl.pallas_call(
        functools.partial(_rwkv_out_ffn_kernel, n_tiles=n_tiles, n_fc=n_fc),
        grid=(b, n_tiles),
        in_specs=token_specs * len(tokens) + [pl.BlockSpec((1, N_MOD, d), lambda bi, ti: (bi, 0, 0))]
        + [_const_spec(c.shape) for c in consts],
        out_specs=pl.BlockSpec((1, tm, d), lambda bi, ti: (bi, ti, 0)),
        out_shape=jax.ShapeDtypeStruct((b, n_tiles * tm, d), F32),
        compiler_params=_cparams(("parallel", "parallel")),
    )(*[a for a in tokens for _ in range(3)], mods, *consts)


def _rwkv_pre_kernel(xm_ref, xp_ref, xn_ref, mod_ref, mu_ref, wr_ref, wk_ref, wv_ref,
                     dw1_ref, dw2_ref, w0_ref, a1_ref, a2_ref, a0_ref, g1_ref, g2_ref,
                     kk_ref, ka_ref, rk_ref, gsum_ref,
                     r_out, v_out, k_out, kk_out, a_out, ld_out, g_out, bv_out,
                     *, n_ctx_tiles, n_tiles):
    tm = xm_ref.shape[1]
    d = xm_ref.shape[2]
    ext = tm + 2 * HALO
    mod = mod_ref[0]
    shift, scale = mod[0:1], mod[1:2]
    keep_prev, keep_next = _edge_keep(pl.program_id(1), n_ctx_tiles, n_tiles)
    h = _modulate(xm_ref[0], shift, scale)
    he = jnp.concatenate([_modulate(xp_ref[0], shift, scale) * keep_prev, h,
                          _modulate(xn_ref[0], shift, scale) * keep_next], axis=0)
    xx = 0.5 * (pltpu.roll(he, 1, 0) + pltpu.roll(he, ext - 1, 0))[HALO:HALO + tm] - h
    mu = mu_ref[...]
    mix = lambda i: h + xx * mu[i:i + 1]
    r = _dot(mix(0), wr_ref[...])
    k = _dot(mix(2), wk_ref[...])
    v = _dot(mix(3), wv_ref[...])
    z = w0_ref[...] + _dot(jnp.tanh(_dot(mix(1), dw1_ref[...])), dw2_ref[...])
    log_decay = -math.exp(-0.5) * _sigmoid(z)
    a = _sigmoid(a0_ref[...] + _dot(_dot(mix(4), a1_ref[...]), a2_ref[...]))
    g = _dot(_sigmoid(_dot(mix(5), g1_ref[...])), g2_ref[...])
    gsum = gsum_ref[...]
    kx = k * kk_ref[...]
    kk = kx / jnp.maximum(jnp.sqrt(_head_sum(kx * kx, gsum)), 1e-12)
    ka = ka_ref[...]
    bonus = (r * rk_ref[...]) * k * (2.0 + (a[:, :d] + a[:, d:] - 2.0) * ka)
    r_out[0] = r
    v_out[0] = v.astype(BF16)
    k_out[0] = k
    kk_out[0] = kk
    a_out[0, 0] = a[:, :d]
    a_out[1, 0] = a[:, d:]
    ld_out[0, 0] = log_decay[:, :d]
    ld_out[1, 0] = log_decay[:, d:]
    g_out[0] = g.astype(BF16)
    bv_out[0] = (_head_sum(bonus, gsum) * v).astype(BF16)


def _block_diag2(m0, m1):
    z = jnp.zeros_like(m0)
    return jnp.concatenate([jnp.concatenate([m0, z], axis=1), jnp.concatenate([z, m1], axis=1)], axis=0)


def _rwkv_pre(xa, mods, p, n_ctx_tiles):
    b, t, d = xa.shape
    tm = RWKV_TILE
    n_tiles = t // tm
    nb = b
    n_ctx = n_ctx_tiles * (TOKEN_TILE // tm)
    mod_map = lambda bi, ti: (jnp.where(ti < n_ctx, nb, bi), 0, 0)
    gate_pad = -p['gate_g1'].shape[1] % LANES
    consts = [
        p['mu'],
        p['w_r'].astype(BF16), p['w_k'].astype(BF16), p['w_v'].astype(BF16),
        jnp.concatenate([p['decay_w1'][0], p['decay_w1'][1]], axis=1).astype(BF16),
        _block_diag2(p['decay_w2'][0], p['decay_w2'][1]).astype(BF16),
        p['decay_w0'].reshape(1, 2 * d),
        jnp.concatenate([p['aaa_a1'][0], p['aaa_a1'][1]], axis=1).astype(BF16),
        _block_diag2(p['aaa_a2'][0], p['aaa_a2'][1]).astype(BF16),
        p['aaa_a0'].reshape(1, 2 * d),
        jnp.pad(p['gate_g1'], ((0, 0), (0, gate_pad))).astype(BF16),
        jnp.pad(p['gate_g2'], ((0, gate_pad), (0, 0))).astype(BF16),
        p['k_k'][None], p['k_a'][None], p['r_k'].reshape(1, d),
        _head_indicator(RWKV_HEAD, 1.0),
    ]
    tok = pl.BlockSpec((1, tm, d), lambda bi, ti: (bi, ti, 0))
    tok2 = pl.BlockSpec((2, 1, tm, d), lambda bi, ti: (0, bi, ti, 0))
    s1 = jax.ShapeDtypeStruct((b, t, d), F32)
    s2 = jax.ShapeDtypeStruct((2, b, t, d), F32)
    h1 = jax.ShapeDtypeStruct((b, t, d), BF16)
    return pl.pallas_call(
        functools.partial(_rwkv_pre_kernel, n_ctx_tiles=n_ctx, n_tiles=n_tiles),
        grid=(b, n_tiles),
        in_specs=_halo_specs(tm, d, t // HALO) + [pl.BlockSpec((1, N_MOD, d), mod_map)]
        + [_const_spec(c.shape) for c in consts],
        out_specs=[tok, tok, tok, tok, tok2, tok2, tok, tok],
        out_shape=[s1, h1, s1, s1, s2, s2, h1, h1],
        compiler_params=_cparams(("parallel", "parallel")),
    )(xa, xa, xa, mods, *consts)


def _head_stack(x, lane_head):
    return jnp.concatenate([jnp.where(lane_head == hh, x, 0.0) for hh in range(HEADS_PER_GROUP)], axis=0)


def _per_head_dot(l, r, lane_head):
    rows = jnp.concatenate([l[:, h * RWKV_HEAD:(h + 1) * RWKV_HEAD] for h in range(HEADS_PER_GROUP)], axis=0)
    full = _dot(rows, r)
    out = None
    for h in range(HEADS_PER_GROUP):
        blk = jnp.where(lane_head == h, full[h * CHUNK:(h + 1) * CHUNK], 0.0)
        out = blk if out is None else out + blk
    return out


def _wkv_kernel(rf_ref, vf_ref, kf_ref, kkf_ref, af_ref, ldf_ref, rb_ref, vb_ref, kb_ref, kkb_ref, ab_ref, ldb_ref,
                ka_ref, yf_ref, yb_ref, state_ref):
    gw = HEADS_PER_GROUP * RWKV_HEAD
    n_groups = rf_ref.shape[2] // gw

    @pl.when(pl.program_id(1) == 0)
    def _():
        state_ref[...] = jnp.zeros(state_ref.shape, F32)

    row = lax.broadcasted_iota(jnp.int32, (CHUNK, gw), 0)
    lane = lax.broadcasted_iota(jnp.int32, (CHUNK, gw), 1)
    col = lane % CHUNK
    lane_head = lane // RWKV_HEAD
    eye = (row == col).astype(F32)
    bd_r = lax.broadcasted_iota(jnp.int32, (gw, gw), 0) // RWKV_HEAD
    bd_c = lax.broadcasted_iota(jnp.int32, (gw, gw), 1) // RWKV_HEAD
    block_diag = bd_r == bd_c
    ka = ka_ref[...]
    stack = lambda x: _head_stack(x, lane_head)
    cols = [slice(gi * gw, (gi + 1) * gw) for gi in range(n_groups)]

    n_sub = rf_ref.shape[1] // CHUNK
    wide = {}
    for di, (refs, incl, last) in enumerate((((rf_ref, vf_ref, kf_ref, kkf_ref, af_ref, ldf_ref), row >= col, CHUNK - 1),
                                             ((rb_ref, vb_ref, kb_ref, kkb_ref, ab_ref, ldb_ref), row <= col, 0))):
        r_ref, v_ref, k_ref, kk_ref, a_ref, ld_ref = refs
        tri = jnp.where(incl[:, :CHUNK], 1.0, 0.0).astype(BF16)
        for wi in range(n_sub):
            ci = wi if di == 0 else n_sub - 1 - wi
            rows = slice(ci * CHUNK, (ci + 1) * CHUNK)
            ld = ld_ref[0, 0, rows]
            ag = a_ref[0, 0, rows]
            kk = kk_ref[0, rows]
            kc = k_ref[0, rows] * (1.0 + (ag - 1.0) * ka)
            bc = kk * ag
            log_p = _dot_split_lhs_rhs(tri, ld)
            log_p_last = log_p[last:last + 1]
            dec_out = jnp.exp(-log_p)
            dec_tail = jnp.exp(log_p_last - log_p)
            wide[di, wi] = dict(
                rows=rows, incl=incl, strict=incl & (row != col), vc=v_ref[0, rows].astype(F32),
                ar=jnp.concatenate([-kk * jnp.exp(log_p - ld), r_ref[0, rows] * jnp.exp(log_p)], axis=0),
                b_out=bc * dec_out, k_out=kc * dec_out,
                bk=jnp.concatenate([bc * dec_tail, kc * dec_tail], axis=0),
                state_decay=jnp.exp(log_p_last))

    chains = [(di, wi, gi) for wi in range(n_sub) for di in range(2) for gi in range(n_groups)]
    w = lambda ch, name: wide[ch[0], ch[1]][name]
    part = lambda ch, name: wide[ch[0], ch[1]][name][:, cols[ch[2]]]
    xb = [_dot_nt(part(ch, 'ar'), stack(part(ch, 'b_out'))) for ch in chains]
    xk = [_dot_nt(part(ch, 'ar'), stack(part(ch, 'k_out'))) for ch in chains]
    a_ab = [jnp.where(w(ch, 'strict'), x[:CHUNK], 0.0) for ch, x in zip(chains, xb)]
    a_rb = [jnp.where(w(ch, 'incl'), x[CHUNK:], 0.0) for ch, x in zip(chains, xb)]
    a_kk = [jnp.concatenate([jnp.where(w(ch, 'strict'), x[:CHUNK], 0.0), jnp.where(w(ch, 'incl'), x[CHUNK:], 0.0)],
                            axis=0) for ch, x in zip(chains, xk)]
    av = [_dot(a, stack(part(ch, 'vc'))) for ch, a in zip(chains, a_kk)]

    m = [eye + jnp.where(row // 2 == col // 2, a, 0.0) for a in a_ab]
    s = 2
    while s < CHUNK:
        couples = (row // (2 * s) == col // (2 * s)) & (row // s != col // s)
        t_off = [_dot(jnp.where(couples, a, 0.0), stack(mm)) for a, mm in zip(a_ab, m)]
        if 4 * s >= CHUNK:
            m = [mm + _per_head_dot(mm, t, lane_head) for mm, t in zip(m, t_off)]
        else:
            m = [mm + _dot(mm, stack(t)) for mm, t in zip(m, t_off)]
        s *= 2

    state = {(di, gi): state_ref[di * n_groups + gi] for di in range(2) for gi in range(n_groups)}
    for wi in range(n_sub):
        idx = [i for i, ch in enumerate(chains) if ch[1] == wi]
        arh = {i: _dot_nt(part(chains[i], 'ar'), state[chains[i][0], chains[i][2]]) for i in idx}
        u = {i: _per_head_dot(m[i], arh[i][:CHUNK] + av[i][:CHUNK], lane_head) for i in idx}
        y = {i: arh[i][CHUNK:] + _per_head_dot(a_rb[i], u[i], lane_head) + av[i][CHUNK:] for i in idx}
        upd = {i: _dot_tn(jnp.concatenate([u[i], part(chains[i], 'vc')], axis=0), part(chains[i], 'bk')) for i in idx}
        for di, y_ref in ((0, yf_ref), (1, yb_ref)):
            y_ref[0, wide[di, wi]['rows']] = jnp.concatenate([y[i] for i in idx if chains[i][0] == di],
                                                             axis=1).astype(y_ref.dtype)
        for i in idx:
            di, _, gi = chains[i]
            state[di, gi] = part(chains[i], 'state_decay') * state[di, gi] + jnp.where(block_diag, upd[i], 0.0)
    for (di, gi), s in state.items():
        state_ref[di * n_groups + gi] = s


def _dot_split_lhs_rhs(tri_bf16, x):
    hi, lo = _split(x)
    return (jnp.dot(tri_bf16, hi, preferred_element_type=F32)
            + jnp.dot(tri_bf16, lo, preferred_element_type=F32))


def _wkv(r, v, k, kk, a, ld, k_a, ctx_len):
    b, t, d = r.shape
    rows = WKV_STEP_CHUNKS * CHUNK
    assert ctx_len % rows == 0 and t % rows == 0
    n_steps = t // rows
    n_ctx = ctx_len // rows

    def back(ci):
        return jnp.where(ci < n_ctx, n_ctx - 1 - ci, n_steps - 1 - (ci - n_ctx))

    fwd = pl.BlockSpec((1, rows, d), lambda bi, ci: (bi, ci, 0))
    bwd = pl.BlockSpec((1, rows, d), lambda bi, ci: (bi, back(ci), 0))
    fwd2 = pl.BlockSpec((1, 1, rows, d), lambda bi, ci: (0, bi, ci, 0))
    bwd2 = pl.BlockSpec((1, 1, rows, d), lambda bi, ci: (1, bi, back(ci), 0))
    gw = HEADS_PER_GROUP * RWKV_HEAD
    y_shape = jax.ShapeDtypeStruct((b, t, d), BF16)
    return pl.pallas_call(
        _wkv_kernel,
        grid=(b, n_steps),
        in_specs=[fwd, fwd, fwd, fwd, fwd2, fwd2, bwd, bwd, bwd, bwd, bwd2, bwd2, _const_spec((1, d))],
        out_specs=[fwd, bwd],
        out_shape=[y_shape, y_shape],
        scratch_shapes=[pltpu.VMEM((2 * d // gw, gw, gw), F32)],
        compiler_params=_cparams(("arbitrary", "arbitrary")),
    )(r, v, k, kk, a, ld, r, v, k, kk, a, ld, k_a[None])


def _rope_tables(seq, ctx_len):
    rows = seq // GRID_W
    row = jnp.repeat(jnp.arange(rows, dtype=F32), GRID_W)
    col = jnp.tile(jnp.arange(GRID_W, dtype=F32), rows)
    n_freq = HEAD_DIM // 4
    inv_freq = ROPE_THETA ** (-jnp.arange(n_freq, dtype=F32) / n_freq)
    ang = jnp.concatenate([row[:, None] * inv_freq, col[:, None] * inv_freq], axis=-1)
    cos = jnp.repeat(jnp.cos(ang), 2, axis=1)
    sin = jnp.repeat(jnp.sin(ang), 2, axis=1) * jnp.tile(jnp.array([-1.0, 1.0], F32), HEAD_DIM // 2)
    cos = jnp.concatenate([jnp.ones((ctx_len, HEAD_DIM), F32), cos], axis=0)
    sin = jnp.concatenate([jnp.zeros((ctx_len, HEAD_DIM), F32), sin], axis=0)
    reps = LANES // HEAD_DIM
    return jnp.tile(cos, (1, reps)), jnp.tile(sin, (1, reps))


def kernel(x, c, ctx, c_ctx, ada_w, ada_b, attn_w_qkv, attn_q_gain, attn_k_gain, attn_w_o, rwkv_mu, rwkv_w_r, rwkv_w_k, rwkv_w_v, rwkv_w_o, rwkv_decay_w0, rwkv_decay_w1, rwkv_decay_w2, rwkv_aaa_a0, rwkv_aaa_a1, rwkv_aaa_a2, rwkv_gate_g1, rwkv_gate_g2, rwkv_k_k, rwkv_k_a, rwkv_r_k, rwkv_gn_w, rwkv_gn_b, ffn_w_up, ffn_conv_w, ffn_conv_b, ffn_w_down, final_gain):
    b, seq, d = x.shape
    ctx_len = ctx.shape[1]
    assert ada_w.shape[0] == 2 and attn_w_qkv.shape[0] == 1 and rwkv_w_r.shape[0] == 1
    assert b + 1 <= MOD_ROWS and ctx_len % TOKEN_TILE == 0 and seq % TOKEN_TILE == 0 and seq % GRID_W == 0
    n_ctx_tiles = ctx_len // TOKEN_TILE

    cc = jnp.zeros((MOD_ROWS, d), F32).at[:b].set(c).at[b].set(c_ctx)
    mods = _ada_mods(cc, ada_w, ada_b)

    cos_t, sin_t = _rope_tables(seq, ctx_len)
    q, kt, v = _attn_pre(ctx, x, mods[0], attn_w_qkv[0], attn_q_gain[0], attn_k_gain[0], cos_t, sin_t, n_ctx_tiles)
    o = _flash(q, kt, v, ctx_len)
    xa = _attn_out(ctx, x, o, mods[0], attn_w_o[0], n_ctx_tiles)
    xa = _conv_ffn(xa, mods[0], ffn_w_up[0], ffn_conv_w[0], ffn_conv_b[0], ffn_w_down[0], n_ctx_tiles, TOKEN_TILE)

    p = dict(mu=rwkv_mu[0], w_r=rwkv_w_r[0], w_k=rwkv_w_k[0], w_v=rwkv_w_v[0],
             decay_w0=rwkv_decay_w0[0], decay_w1=rwkv_decay_w1[0], decay_w2=rwkv_decay_w2[0],
             aaa_a0=rwkv_aaa_a0[0], aaa_a1=rwkv_aaa_a1[0], aaa_a2=rwkv_aaa_a2[0],
             gate_g1=rwkv_gate_g1[0], gate_g2=rwkv_gate_g2[0],
             k_k=rwkv_k_k[0], k_a=rwkv_k_a[0], r_k=rwkv_r_k[0])
    r, vv, k, kk, a, ld, g, bv = _rwkv_pre(xa, mods[1], p, n_ctx_tiles)
    y = _wkv(r, vv, k, kk, a, ld, rwkv_k_a[0], ctx_len)
    return _rwkv_out_conv_ffn(xa, y, bv, g, mods[1], rwkv_gn_w[0], rwkv_gn_b[0], rwkv_w_o[0],
                              ffn_w_up[1], ffn_conv_w[1], ffn_conv_b[1], ffn_w_down[1], final_gain, n_ctx_tiles)
```
